```python
import math
import jax
import jax.numpy as jnp
from jax import lax
import numpy as np

D_MODEL = 1024
BATCH = 8
SEQ = 2048
DEPTH = 4

GRID_W = 64
CTX_LEN = 256

DN_HEADS = 4
DN_HEAD_DIM = 128
DN_WIDTH = DN_HEADS * DN_HEAD_DIM
DN_CHUNK = 64
DN_CONV_W = 3
HY_WIDTH = 256
HY_BANDS = 16
HY_EMB = 1 + 2 * HY_BANDS
HY_ORDER = 64
HY_CONV_W = 3
HY_TARGET = 1e-2
HY_FAST = 0.3
HY_SLOW = 1.5
SC_WIDTH = 256
SC_CONV_W = 3
N_BRANCH = 3
A_COLS = 4 * DN_WIDTH + 4 * DN_HEADS
B_OFF = A_COLS
C_OFF = B_OFF + 3 * HY_WIDTH
G_OFF = C_OFF + 3 * SC_WIDTH
N_IN = G_OFF + N_BRANCH * D_MODEL
N_EXPERTS = 16
CAPACITY_FACTOR = 2
EXPERT_FF = 768
NORM_EPS = 1e-6

kernel_name = "hybrid_deltanet_hyena_shortconv_ecmoe_diffusion"


def rmsnorm(x, g):
    xf = x.astype(jnp.float32)
    y = xf * lax.rsqrt(jnp.mean(xf * xf, axis=-1, keepdims=True) + NORM_EPS)
    return (y * g.astype(jnp.float32)).astype(x.dtype)


def l2norm(t):
    return t * lax.rsqrt(jnp.sum(t * t, axis=-1, keepdims=True) + 1e-6)


def short_conv(u, w, rows):
    k = w.shape[0]
    pad = k // 2
    b, l, ch = u.shape
    v = u if rows is None else u.reshape(b, rows, GRID_W, ch)
    n = v.shape[-2]
    vp = jnp.pad(v, [(0, 0)] * (v.ndim - 2) + [(pad, pad), (0, 0)])
    y = sum(vp[..., j:j + n, :] * w[j] for j in range(k))
    return y.reshape(b, l, ch)


def gated_delta_chunked(q, k, v, g, beta, s0):
    b, l, h, _ = q.shape
    dv = v.shape[-1]
    n = l // DN_CHUNK

    def blk(t):
        t = t.reshape((b, n, DN_CHUNK, h) + t.shape[3:])
        return jnp.moveaxis(t, 3, 1)

    q, k, v, g, beta = [blk(t) for t in (q, k, v, g, beta)]
    gc = jnp.cumsum(g, axis=-1)
    incl = jnp.tril(jnp.ones((DN_CHUNK, DN_CHUNK), bool))
    strict = jnp.tril(jnp.ones((DN_CHUNK, DN_CHUNK), bool), -1)
    diff = gc[..., :, None] - gc[..., None, :]
    decay = jnp.where(incl, jnp.exp(jnp.where(incl, diff, 0.0)), 0.0)
    kb = k * beta[..., None]
    m = jnp.where(strict, jnp.einsum('bhncd,bhnsd->bhncs', kb, k) * decay, 0.0)
    a = m + jnp.eye(DN_CHUNK, dtype=m.dtype)
    u = lax.linalg.triangular_solve(a, v * beta[..., None], left_side=True, lower=True, unit_diagonal=True)
    w = lax.linalg.triangular_solve(a, kb * jnp.exp(gc)[..., None], left_side=True, lower=True,
                                    unit_diagonal=True)
    attn = jnp.where(incl, jnp.einsum('bhncd,bhnsd->bhncs', q, k) * decay, 0.0)
    q_dec = q * jnp.exp(gc)[..., None]
    g_last = gc[..., -1]
    k_dec = k * jnp.exp(g_last[..., None] - gc)[..., None]

    def step(s, inp):
        qd, kd, ui, wi, ai, gl = inp
        v_new = ui - jnp.einsum('bhck,bhkv->bhcv', wi, s)
        o = jnp.einsum('bhck,bhkv->bhcv', qd, s) + jnp.einsum('bhcs,bhsv->bhcv', ai, v_new)
        s = s * jnp.exp(gl)[..., None, None] + jnp.einsum('bhck,bhcv->bhkv', kd, v_new)
        return s, o

    xs = tuple(jnp.moveaxis(t, 2, 0) for t in (q_dec, k_dec, u, w, attn, g_last))
    s_fin, o = lax.scan(step, s0, xs)
    o = jnp.moveaxis(o, 0, 2).reshape(b, h, l, dv)
    return jnp.moveaxis(o, 1, 2), s_fin


def deltanet_branch(pa, lp, s0_f, s0_b, rows):
    b, l, _ = pa.shape
    f32 = jnp.float32
    qkv = jax.nn.silu(short_conv(pa[..., :3 * DN_WIDTH], lp['dn_conv_w'], rows)).astype(f32)
    q, k, v = [t.reshape(b, l, DN_HEADS, DN_HEAD_DIM) for t in jnp.split(qkv, 3, axis=-1)]
    q = l2norm(q) * (DN_HEAD_DIM ** -0.5)
    k = l2norm(k)
    z = pa[..., 3 * DN_WIDTH:4 * DN_WIDTH].astype(f32).reshape(b, l, DN_HEADS, DN_HEAD_DIM)
    gcols = pa[..., 4 * DN_WIDTH:].astype(f32).reshape(b, l, 2, 2, DN_HEADS)
    beta = jax.nn.sigmoid(gcols[:, :, 0])
    g = -jnp.exp(lp['dn_a_log'].astype(f32)) * jax.nn.softplus(gcols[:, :, 1] + lp['dn_dt_bias'].astype(f32))
    o_f, s_f = gated_delta_chunked(q, k, v, g[:, :, 0], beta[:, :, 0], s0_f)
    flip = lambda t: jnp.flip(t, axis=1)
    o_b, s_b = gated_delta_chunked(flip(q), flip(k), flip(v), flip(g[:, :, 1]), flip(beta[:, :, 1]), s0_b)
    o = rmsnorm(o_f + flip(o_b), lp['dn_norm_g']) * jax.nn.silu(z)
    return o.reshape(b, l, DN_WIDTH).astype(pa.dtype), s_f, s_b


def hyena_filter(l, lp):
    f32 = jnp.float32
    pos = jnp.arange(l, dtype=f32)
    t = (pos / max(l - 1, 1))[:, None]
    bands = jnp.linspace(1e-4, HY_BANDS - 1, HY_BANDS, dtype=f32)
    ang = (2.0 * math.pi / l) * pos[:, None] * bands[None, :]
    feats = jnp.concatenate([t, jnp.cos(ang), -jnp.sin(ang)], axis=-1)
    freq = lp['hy_freq'].astype(f32)
    hh = jnp.sin(freq * (feats @ lp['hy_w1'].astype(f32) + lp['hy_b1'].astype(f32)))
    hh = jnp.sin(freq * (hh @ lp['hy_w2'].astype(f32) + lp['hy_b2'].astype(f32)))
    hh = hh @ lp['hy_w3'].astype(f32)
    deltas = jnp.abs(jnp.linspace(math.log(HY_TARGET) / HY_SLOW, math.log(HY_TARGET) / HY_FAST,
                                  HY_WIDTH, dtype=f32))
    window = jnp.exp(-t * deltas[None, :])
    h_fwd = hh[:, :HY_WIDTH] * window
    h_bwd = hh[:, HY_WIDTH:] * window
    return jnp.concatenate([h_fwd, jnp.zeros((1, HY_WIDTH), f32), h_bwd[:0:-1]], axis=0)


def hyena_branch(pb, lp, rows):
    u = (short_conv(pb, lp['hy_conv_w'], rows) + lp['hy_conv_b']).astype(jnp.float32)
    x0, x1, v = jnp.split(u, 3, axis=-1)
    z = x1 * v
    l = z.shape[1]
    kern = hyena_filter(l, lp)
    y = jnp.fft.irfft(jnp.fft.rfft(z, n=2 * l, axis=1) * jnp.fft.rfft(kern, axis=0)[None],
                      n=2 * l, axis=1)[:, :l]
    y = x0 * (y + z * lp['hy_d_bias'].astype(jnp.float32))
    return y.astype(pb.dtype)


def shortconv_branch(pc, lp, rows):
    bg, cg, hx = jnp.split(pc, 3, axis=-1)
    return bg * short_conv(cg * hx, lp['sc_conv_w'], rows)


def token_mixer(h, lp, s0_f, s0_b, rows):
    p = h @ lp['w_in']
    ya, s_f, s_b = deltanet_branch(p[..., :A_COLS], lp, s0_f, s0_b, rows)
    yb = hyena_branch(p[..., B_OFF:C_OFF], lp, rows)
    yc = shortconv_branch(p[..., C_OFF:G_OFF], lp, rows)
    gates = jax.nn.sigmoid(p[..., G_OFF:]).reshape(p.shape[:2] + (N_BRANCH, D_MODEL))
    merged = (gates[..., 0, :] * (ya @ lp['w_branch_a'])
              + gates[..., 1, :] * (yb @ lp['w_branch_b'])
              + gates[..., 2, :] * (yc @ lp['w_branch_c']))
    return merged @ lp['w_out'], s_f, s_b


def expert_choice_moe(h, router_w, w_gate, w_up, w_down):
    b, n, _ = h.shape
    cap = CAPACITY_FACTOR * n // N_EXPERTS
    aff = jax.nn.softmax((h @ router_w).astype(jnp.float32), axis=-1)
    top_val, top_idx = lax.top_k(jnp.swapaxes(aff, 1, 2), cap)
    bidx = jnp.arange(b)[:, None, None]
    xe = h[bidx, top_idx]
    hid = jax.nn.silu(jnp.einsum('becd,edf->becf', xe, w_gate)) * jnp.einsum('becd,edf->becf', xe, w_up)
    ye = jnp.einsum('becf,efd->becd', hid, w_down) * top_val[..., None].astype(h.dtype)
    return jnp.zeros_like(h).at[bidx, top_idx].add(ye.astype(h.dtype))


def setup_inputs(seed: int = 0) -> dict:
    key = jax.random.key(seed)
    ks = jax.random.split(key, 40)
    D = D_MODEL

    def nrm(i, shape, scale):
        return jax.random.normal(ks[i], shape, jnp.float32) * scale

    dt = jnp.exp(jax.random.uniform(ks[10], (DEPTH, 2, DN_HEADS), jnp.float32, math.log(1e-3), math.log(1e-1)))
    return {
        "x": nrm(0, (BATCH, SEQ, D), 1.0),
        "c": nrm(1, (BATCH, D), 1.0),
        "ctx": nrm(2, (BATCH, CTX_LEN, D), 1.0),
        "c_ctx": nrm(3, (D,), 1.0),
        "ada_w": nrm(4, (DEPTH, D, 6 * D), D ** -0.5),
        "ada_b": nrm(5, (DEPTH, 6 * D), 0.02),
        "norm1_g": 1.0 + nrm(6, (DEPTH, D), 0.02),
        "norm2_g": 1.0 + nrm(7, (DEPTH, D), 0.02),
        "w_in": nrm(8, (DEPTH, D, N_IN), D ** -0.5),
        "dn_conv_w": nrm(9, (DEPTH, DN_CONV_W, 3 * DN_WIDTH), DN_CONV_W ** -0.5),
        "dn_a_log": jnp.log(jax.random.uniform(ks[11], (DEPTH, 2, DN_HEADS), jnp.float32, 1.0, 16.0)),
        "dn_dt_bias": dt + jnp.log(-jnp.expm1(-dt)),
        "dn_norm_g": 1.0 + nrm(12, (DEPTH, DN_HEAD_DIM), 0.02),
        "hy_conv_w": nrm(13, (DEPTH, HY_CONV_W, 3 * HY_WIDTH), HY_CONV_W ** -0.5),
        "hy_conv_b": nrm(14, (DEPTH, 3 * HY_WIDTH), 0.02),
        "hy_w1": nrm(15, (DEPTH, HY_EMB, HY_ORDER), HY_EMB ** -0.5),
        "hy_b1": nrm(16, (DEPTH, HY_ORDER), 0.1),
        "hy_w2": nrm(17, (DEPTH, HY_ORDER, HY_ORDER), HY_ORDER ** -0.5),
        "hy_b2": nrm(18, (DEPTH, HY_ORDER), 0.1),
        "hy_w3": nrm(19, (DEPTH, HY_ORDER, 2 * HY_WIDTH), 0.02),
        "hy_freq": 1.0 + nrm(20, (DEPTH, HY_ORDER), 0.02),
        "hy_d_bias": nrm(21, (DEPTH, HY_WIDTH), 1.0),
        "sc_conv_w": nrm(22, (DEPTH, SC_CONV_W, SC_WIDTH), SC_CONV_W ** -0.5),
        "w_branch_a": nrm(23, (DEPTH, DN_WIDTH, D), DN_WIDTH ** -0.5),
        "w_branch_b": nrm(24, (DEPTH, HY_WIDTH, D), HY_WIDTH ** -0.5),
        "w_branch_c": nrm(25, (DEPTH, SC_WIDTH, D), SC_WIDTH ** -0.5),
        "w_out": nrm(26, (DEPTH, D, D), D ** -0.5),
        "router_w": nrm(27, (DEPTH, D, N_EXPERTS), D ** -0.5),
        "exp_w_gate": nrm(28, (DEPTH, N_EXPERTS, D, EXPERT_FF), D ** -0.5),
        "exp_w_up": nrm(29, (DEPTH, N_EXPERTS, D, EXPERT_FF), D ** -0.5),
        "exp_w_down": nrm(30, (DEPTH, N_EXPERTS, EXPERT_FF, D), EXPERT_FF ** -0.5),
        "final_norm_g": 1.0 + nrm(31, (D,), 0.02),
    }


def reference(x, c, ctx, c_ctx, ada_w, ada_b, norm1_g, norm2_g, w_in, dn_conv_w, dn_a_log, dn_dt_bias,
              dn_norm_g, hy_conv_w, hy_conv_b, hy_w1, hy_b1, hy_w2, hy_b2, hy_w3, hy_freq, hy_d_bias,
              sc_conv_w, w_branch_a, w_branch_b, w_branch_c, w_out, router_w, exp_w_gate, exp_w_up,
              exp_w_down, final_norm_g):
    b = x.shape[0]
    rows = x.shape[1] // GRID_W
    silu_c = jax.nn.silu(c)
    silu_cc = jax.nn.silu(c_ctx)
    s0 = jnp.zeros((b, DN_HEADS, DN_HEAD_DIM, DN_HEAD_DIM), jnp.float32)
    for i in range(DEPTH):
        lp = {
            'w_in': w_in[i], 'dn_conv_w': dn_conv_w[i], 'dn_a_log': dn_a_log[i], 'dn_dt_bias': dn_dt_bias[i],
            'dn_norm_g': dn_norm_g[i], 'hy_conv_w': hy_conv_w[i], 'hy_conv_b': hy_conv_b[i],
            'hy_w1': hy_w1[i], 'hy_b1': hy_b1[i], 'hy_w2': hy_w2[i], 'hy_b2': hy_b2[i], 'hy_w3': hy_w3[i],
            'hy_freq': hy_freq[i], 'hy_d_bias': hy_d_bias[i], 'sc_conv_w': sc_conv_w[i],
            'w_branch_a': w_branch_a[i], 'w_branch_b': w_branch_b[i], 'w_branch_c': w_branch_c[i],
            'w_out': w_out[i],
        }
        mod_x = (silu_c @ ada_w[i] + ada_b[i])[:, None, :]
        mod_c = silu_cc @ ada_w[i] + ada_b[i]
        sh1x, sc1x, g1x, sh2x, sc2x, g2x = jnp.split(mod_x, 6, axis=-1)
        sh1c, sc1c, g1c, sh2c, sc2c, g2c = jnp.split(mod_c, 6, axis=-1)

        hc = rmsnorm(ctx, norm1_g[i]) * (1.0 + sc1c) + sh1c
        if i == DEPTH - 1:
            _, s_f, s_b = deltanet_branch(hc @ w_in[i][:, :A_COLS], lp, s0, s0, None)
        else:
            mix_c, s_f, s_b = token_mixer(hc, lp, s0, s0, None)
            ctx = ctx + g1c * mix_c
            hc2 = rmsnorm(ctx, norm2_g[i]) * (1.0 + sc2c) + sh2c
            ctx = ctx + g2c * expert_choice_moe(hc2, router_w[i], exp_w_gate[i], exp_w_up[i], exp_w_down[i])

        hx = rmsnorm(x, norm1_g[i]) * (1.0 + sc1x) + sh1x
        mix_x, _, _ = token_mixer(hx, lp, s_f, s_b, rows)
        x = x + g1x * mix_x
        hx2 = rmsnorm(x, norm2_g[i]) * (1.0 + sc2x) + sh2x
        x = x + g2x * expert_choice_moe(hx2, router_w[i], exp_w_gate[i], exp_w_up[i], exp_w_down[i])
    return rmsnorm(x, final_norm_g)
```

```python
import functools
import math

import jax
import jax.numpy as jnp
import numpy as np
from jax import lax
from jax.experimental import pallas as pl
from jax.experimental.pallas import tpu as pltpu

F32 = jnp.float32
BF16 = jnp.bfloat16
HIGHEST = lax.Precision.HIGHEST

GRID_W = 64
DN_HEADS = 4
DN_HEAD_DIM = 128
DN_WIDTH = DN_HEADS * DN_HEAD_DIM
DN_CHUNK = 64
HY_WIDTH = 256
HY_BANDS = 16
HY_TARGET = 1e-2
HY_FAST = 0.3
HY_SLOW = 1.5
SC_WIDTH = 256
N_BRANCH = 3
N_EXPERTS = 16
CAPACITY_FACTOR = 2
NORM_EPS = 1e-6

LANES = 128
VMEM_LIMIT = 56 * 1024 * 1024


def _cparams(sem):
    return pltpu.CompilerParams(dimension_semantics=sem, vmem_limit_bytes=VMEM_LIMIT)


def _mm(a, b):
    return jnp.dot(a.astype(BF16), b.astype(BF16), preferred_element_type=F32)


def _mm_nt(a, b):
    return lax.dot_general(a.astype(BF16), b.astype(BF16), (((1,), (1,)), ((), ())),
                           preferred_element_type=F32)


def _mm_tn(a, b):
    return lax.dot_general(a.astype(BF16), b.astype(BF16), (((0,), (0,)), ((), ())),
                           preferred_element_type=F32)


def _mm_f32(a, b):
    return jnp.dot(a, b, precision=HIGHEST, preferred_element_type=F32)


def _mm_split(a_bf16, b):
    hi = b.astype(BF16)
    lo = (b - hi.astype(F32)).astype(BF16)
    return (jnp.dot(a_bf16, hi, preferred_element_type=F32)
            + jnp.dot(a_bf16, lo, preferred_element_type=F32))


def _sigmoid(x):
    return 0.5 * jnp.tanh(0.5 * x) + 0.5


def _silu(x):
    return x * _sigmoid(x)


def _softplus(x):
    return jnp.maximum(x, 0.0) + jnp.log(1.0 + jnp.exp(-jnp.abs(x)))


def _conv3(x, w, row_len):
    t = x.shape[0]
    pos = lax.broadcasted_iota(jnp.int32, x.shape, 0) & (row_len - 1)
    prev = jnp.where(pos == 0, 0.0, pltpu.roll(x, 1, axis=0))
    nxt = jnp.where(pos == row_len - 1, 0.0, pltpu.roll(x, t - 1, axis=0))
    return prev * w[0:1] + x * w[1:2] + nxt * w[2:3]


def _mod_kernel(c_ref, w_ref, b_ref, o_ref):
    o_ref[0] = _mm_f32(_silu(c_ref[...]), w_ref[0]) + b_ref[0]


def _modulation(cvec, ada_w, ada_b):
    depth, d, n6 = ada_w.shape
    rows = cvec.shape[0]
    tn = 1536
    return pl.pallas_call(
        _mod_kernel,
        grid=(depth, n6 // tn),
        in_specs=[pl.BlockSpec((rows, d), lambda i, j: (0, 0)),
                  pl.BlockSpec((1, d, tn), lambda i, j: (i, 0, j)),
                  pl.BlockSpec((1, 1, tn), lambda i, j: (i, 0, j))],
        out_specs=pl.BlockSpec((1, rows, tn), lambda i, j: (i, 0, j)),
        out_shape=jax.ShapeDtypeStruct((depth, rows, n6), F32),
        compiler_params=_cparams(("parallel", "parallel")),
        name="modulation",
    )(cvec, ada_w, ada_b.reshape(depth, 1, n6))


_QKV = (0, 3 * DN_WIDTH)
_Z = (_QKV[0] + _QKV[1], DN_WIDTH)
_HY = (_Z[0] + _Z[1], 3 * HY_WIDTH)
_SC = (_HY[0] + _HY[1], 3 * SC_WIDTH)
_GATES = (_SC[0] + _SC[1], N_BRANCH * 1024)
_BA = (_GATES[0] + _GATES[1], LANES)
N_PACKED = _BA[0] + _BA[1]


def _pack_w_in(w_in):
    d = w_in.shape[0]
    a = 4 * DN_WIDTH
    nba = 4 * DN_HEADS
    qkvz = w_in[:, :a]
    ba = w_in[:, a:a + nba]
    rest = w_in[:, a + nba:]
    pad = jnp.zeros((d, LANES - nba), w_in.dtype)
    return jnp.concatenate([qkvz, rest, ba, pad], axis=1).astype(BF16)


def _rms_mod(x, g, scale, shift):
    y = x * lax.rsqrt(jnp.mean(x * x, axis=-1, keepdims=True) + NORM_EPS)
    return (y * g) * (1.0 + scale) + shift


def _inproj_kernel(*refs, has_moe):
    if has_moe:
        x_ref, moe_ref, modp_ref, mod_ref, g_ref, w_ref = refs[:6]
        outs = refs[6:]
        xo_ref = outs[-1]
        x = x_ref[0] + modp_ref[0][5:6] * moe_ref[0]
        xo_ref[0] = x
    else:
        x_ref, mod_ref, g_ref, w_ref = refs[:4]
        outs = refs[4:]
        x = x_ref[0]
    qkv_ref, z_ref, hy_ref, sc_ref, gates_ref, ba_ref = outs[:6]
    m = mod_ref[0]
    hb = _rms_mod(x, g_ref[...], m[1:2], m[0:1]).astype(BF16)

    def proj(sec, lo=0, width=None):
        a = sec[0] + lo
        wd = sec[1] if width is None else width
        return jnp.dot(hb, w_ref[:, a:a + wd], preferred_element_type=F32)

    qkv_ref[0] = proj(_QKV).astype(BF16)
    z_ref[0] = proj(_Z).astype(BF16)
    hy_ref[0] = proj(_HY).astype(BF16)
    sc_ref[0] = proj(_SC).astype(BF16)
    for j in range(N_BRANCH):
        gates_ref[0, :, j * 1024:(j + 1) * 1024] = proj(_GATES, j * 1024, 1024).astype(BF16)
    ba_ref[0] = proj(_BA)


def _inproj(x, mod, norm_g, w_packed, tm, moe=None, mod_prev=None):
    b, l, d = x.shape
    has_moe = moe is not None
    tok = pl.BlockSpec((1, tm, d), lambda i, j: (i, j, 0))
    modspec = pl.BlockSpec((1, 8, d), lambda i, j: (i, 0, 0))
    in_specs = [tok]
    args = [x]
    if has_moe:
        in_specs += [tok, modspec]
        args += [moe, mod_prev]
    in_specs += [modspec,
                 pl.BlockSpec((1, d), lambda i, j: (0, 0)),
                 pl.BlockSpec((d, N_PACKED), lambda i, j: (0, 0), pipeline_mode=pl.Buffered(1))]
    args += [mod, norm_g.reshape(1, d), w_packed]
    widths = [(_QKV[1], BF16), (_Z[1], BF16), (_HY[1], BF16), (_SC[1], BF16), (_GATES[1], BF16), (_BA[1], F32)]
    out_specs = [pl.BlockSpec((1, tm, w), lambda i, j: (i, j, 0)) for w, _ in widths]
    out_shape = [jax.ShapeDtypeStruct((b, l, w), dt) for w, dt in widths]
    if has_moe:
        out_specs.append(tok)
        out_shape.append(jax.ShapeDtypeStruct((b, l, d), F32))
    return pl.pallas_call(
        functools.partial(_inproj_kernel, has_moe=has_moe),
        grid=(b, l // tm),
        in_specs=in_specs,
        out_specs=out_specs,
        out_shape=out_shape,
        compiler_params=_cparams(("parallel", "parallel")),
        name="inproj",
    )(*args)


_BETA_LANE = 0
_DECAY_LANE = 2 * DN_HEADS
_BWD_DECAY_LANE = _DECAY_LANE + DN_HEADS


def _dn_gates_kernel(ba_ref, alog_ref, dtb_ref, o_ref, *, nc):
    x = ba_ref[0]
    beta = _sigmoid(x)
    g = -jnp.exp(alog_ref[...]) * _softplus(x + dtb_ref[...])
    ii = lax.broadcasted_iota(jnp.int32, (DN_CHUNK, DN_CHUNK), 0)
    jj = lax.broadcasted_iota(jnp.int32, (DN_CHUNK, DN_CHUNK), 1)
    lower = jnp.where(ii >= jj, 1.0, 0.0).astype(F32)
    upper = jnp.where(ii <= jj, 1.0, 0.0).astype(F32)
    lane = lax.broadcasted_iota(jnp.int32, (DN_CHUNK, LANES), 1)
    for c in range(nc):
        rows = slice(c * DN_CHUNK, (c + 1) * DN_CHUNK)
        gch = g[rows]
        acc = jnp.where(lane >= _BWD_DECAY_LANE, _mm_f32(upper, gch), _mm_f32(lower, gch))
        o_ref[0, rows, :] = jnp.where(lane < _DECAY_LANE, beta[rows], acc)


def _dn_gates(ba, a_log, dt_bias):
    b, l, _ = ba.shape
    nc = l // DN_CHUNK
    vec = lambda p: jnp.zeros((1, LANES), F32).at[0, _DECAY_LANE:_DECAY_LANE + 2 * DN_HEADS].set(p.reshape(-1))
    return pl.pallas_call(
        functools.partial(_dn_gates_kernel, nc=nc),
        grid=(b,),
        in_specs=[pl.BlockSpec((1, l, LANES), lambda i: (i, 0, 0)),
                  pl.BlockSpec((1, LANES), lambda i: (0, 0)),
                  pl.BlockSpec((1, LANES), lambda i: (0, 0))],
        out_specs=pl.BlockSpec((1, l, LANES), lambda i: (i, 0, 0)),
        out_shape=jax.ShapeDtypeStruct((b, l, LANES), F32),
        compiler_params=_cparams(("parallel",)),
        name="dn_gates",
    )(ba, vec(a_log), vec(dt_bias))


_GROUP_CHUNKS = 4


def _dn_kernel(qkv_ref, z_ref, gb_ref, gbt_ref, cw_ref, ng_ref, s0f_ref, s0b_ref, ya_ref, sf_ref, sb_ref,
               q_s, k_s, v_s, of_s, ob_s, st_s, *, ngroups, gc, row_len):
    c, hd, nh = DN_CHUNK, DN_HEAD_DIM, DN_HEADS
    gt = gc * c
    shift = c.bit_length() - 1

    for part, dst in enumerate((q_s, k_s, v_s)):
        for h in range(nh):
            src = slice((part * nh + h) * hd, (part * nh + h + 1) * hd)
            t = _silu(_conv3(qkv_ref[0, :, src].astype(F32), cw_ref[:, src], row_len))
            if part < 2:
                t = t * lax.rsqrt(jnp.sum(t * t, axis=-1, keepdims=True) + 1e-6)
            if part == 0:
                t = t * (hd ** -0.5)
            dst[:, h * hd:(h + 1) * hd] = t
    for h in range(nh):
        st_s[h] = s0f_ref[0, h]
        st_s[nh + h] = s0b_ref[0, h]

    rr = lax.broadcasted_iota(jnp.int32, (c, gt), 0)
    ln = lax.broadcasted_iota(jnp.int32, (c, gt), 1)
    lblk = ln >> shift
    cc = ln & (c - 1)
    eye_p = jnp.where(rr == cc, 1.0, 0.0).astype(F32)
    inside = [(rr >> lv) == (cc >> lv) for lv in range(1, shift + 1)]
    level_masks = [jnp.where(inside[0], 1.0, 0.0).astype(F32)] + [
        jnp.where(jnp.logical_and(inside[j], jnp.logical_not(inside[j - 1])), 1.0, 0.0).astype(F32)
        for j in range(1, shift)]
    same_blk = ((lax.broadcasted_iota(jnp.int32, (gt, gt), 0) >> shift)
                == (lax.broadcasted_iota(jnp.int32, (gt, gt), 1) >> shift))
    blk_mask = jnp.where(same_blk, 1.0, 0.0).astype(BF16)
    rblk = lax.broadcasted_iota(jnp.int32, (gt, 1), 0) >> shift

    def pack_cols(col):
        out = jnp.broadcast_to(col[0:c], (c, gt))
        for i in range(1, gc):
            out = jnp.where(lblk == i, jnp.broadcast_to(col[i * c:(i + 1) * c], (c, gt)), out)
        return out

    def pack_diag(full):
        out = full[0:c]
        for i in range(1, gc):
            out = jnp.where(lblk == i, full[i * c:(i + 1) * c], out)
        return out

    def block_diag(packed_bf16):
        return jnp.concatenate([packed_bf16] * gc, axis=0) * blk_mask

    def dotf(a, b):
        return jnp.dot(a, b, preferred_element_type=F32)

    def prepare(h, d, g):
        r0 = pl.multiple_of(g * gt, gt)
        hs = slice(h * hd, (h + 1) * hd)
        qg = q_s[pl.ds(r0, gt), hs]
        kg = k_s[pl.ds(r0, gt), hs]
        vg = v_s[pl.ds(r0, gt), hs]
        gbg = gb_ref[0, pl.ds(r0, gt), :]
        col = d * nh + h
        beta = gbg[:, _BETA_LANE + col:_BETA_LANE + col + 1]
        gcc = gbg[:, _DECAY_LANE + col:_DECAY_LANE + col + 1]
        gcr = gbt_ref[0, g][_DECAY_LANE + col:_DECAY_LANE + col + 1, :]
        if d == 0:
            incl, strict, last = rr >= cc, rr > cc, c - 1
        else:
            incl, strict, last = rr <= cc, rr < cc, 0
        kq = _mm_nt(jnp.concatenate([kg, qg], axis=0), kg)
        decay = jnp.where(incl, jnp.exp(jnp.where(incl, pack_cols(gcc) - gcr, 0.0)), 0.0)
        m = jnp.where(strict, pack_diag(kq[:gt]) * pack_cols(beta) * decay, 0.0)
        attn = pack_diag(kq[gt:]) * decay
        eg = jnp.exp(gcc)
        glast = jnp.broadcast_to(gcc[last:last + 1], (gt, 1))
        for i in range(1, gc):
            glast = jnp.where(rblk == i, gcc[i * c + last:i * c + last + 1], glast)
        return dict(m=m, attn=attn.astype(BF16), qd=(qg * eg).astype(BF16),
                    kd=(kg * jnp.exp(glast - gcc)).astype(BF16),
                    rhs=jnp.concatenate([vg * beta, kg * (beta * eg)], axis=1).astype(BF16),
                    chunk_decay=[jnp.exp(gcc[i * c + last:i * c + last + 1]) for i in range(gc)])

    def body(j, carry):
        groups = (j, ngroups - 1 - j)
        slots = [(h, d) for h in range(nh) for d in range(2)]
        state = [st_s[d * nh + h] for h, d in slots]
        pre = [prepare(h, d, groups[d]) for h, d in slots]
        ts = [eye_p - p['m'] * level_masks[0] for p in pre]
        for level_mask in level_masks[1:]:
            xs = [dotf((p['m'] * level_mask).astype(BF16), block_diag(t.astype(BF16))) for p, t in zip(pre, ts)]
            ts = [t - dotf(t.astype(BF16), block_diag(x.astype(BF16))) for t, x in zip(ts, xs)]
        uws = [dotf(block_diag(t.astype(BF16)), p['rhs']) for p, t in zip(pre, ts)]
        outs = [[None] * gc for _ in slots]
        for step in range(gc):
            for n, (h, d) in enumerate(slots):
                i = step if d == 0 else gc - 1 - step
                rows = slice(i * c, (i + 1) * c)
                p, uw, s = pre[n], uws[n], state[n]
                wq = jnp.concatenate([uw[rows, hd:].astype(BF16), p['qd'][rows]], axis=0)
                a = dotf(wq, s.astype(BF16))
                v_new = (uw[rows, :hd] - a[:c]).astype(BF16)
                outs[n][i] = a[c:] + dotf(p['attn'][:, rows], v_new)
                state[n] = s * p['chunk_decay'][i] + lax.dot_general(
                    p['kd'][rows], v_new, (((0,), (0,)), ((), ())), preferred_element_type=F32)
        for n, (h, d) in enumerate(slots):
            st_s[d * nh + h] = state[n]
            r0 = pl.multiple_of(groups[d] * gt, gt)
            (of_s, ob_s)[d][pl.ds(r0, gt), h * hd:(h + 1) * hd] = jnp.concatenate(outs[n], axis=0)
        return carry

    lax.fori_loop(0, ngroups, body, 0)
    for h in range(nh):
        sf_ref[0, h] = st_s[h]
        sb_ref[0, h] = st_s[nh + h]
        hs = slice(h * hd, (h + 1) * hd)
        o = of_s[:, hs] + ob_s[:, hs]
        y = o * lax.rsqrt(jnp.mean(o * o, axis=-1, keepdims=True) + NORM_EPS) * ng_ref[...]
        ya_ref[0, :, hs] = (y * _silu(z_ref[0, :, hs].astype(F32))).astype(BF16)


def _deltanet(qkv, z, gb, conv_w, norm_g, s0f, s0b, row_len):
    b, l, _ = qkv.shape
    c, hd, nh = DN_CHUNK, DN_HEAD_DIM, DN_HEADS
    gc = min(_GROUP_CHUNKS, l // c)
    gt = gc * c
    ngroups = l // gt
    nlanes = 4 * nh
    gbt = jnp.transpose(gb[:, :, :nlanes].reshape(b, ngroups, gt, nlanes), (0, 1, 3, 2))
    st_blk = pl.BlockSpec((1, nh, hd, hd), lambda i: (i, 0, 0, 0))
    seq_blk = lambda w: pl.BlockSpec((1, l, w), lambda i: (i, 0, 0))
    return pl.pallas_call(
        functools.partial(_dn_kernel, ngroups=ngroups, gc=gc, row_len=row_len),
        grid=(b,),
        in_specs=[seq_blk(3 * DN_WIDTH), seq_blk(DN_WIDTH), seq_blk(LANES),
                  pl.BlockSpec((1, ngroups, nlanes, gt), lambda i: (i, 0, 0, 0)),
                  pl.BlockSpec((3, 3 * DN_WIDTH), lambda i: (0, 0)),
                  pl.BlockSpec((1, hd), lambda i: (0, 0)),
                  st_blk, st_blk],
        out_specs=[seq_blk(DN_WIDTH), st_blk, st_blk],
        out_shape=[jax.ShapeDtypeStruct((b, l, DN_WIDTH), BF16),
                   jax.ShapeDtypeStruct((b, nh, hd, hd), F32),
                   jax.ShapeDtypeStruct((b, nh, hd, hd), F32)],
        scratch_shapes=[pltpu.VMEM((l, DN_WIDTH), F32) for _ in range(5)] + [
            pltpu.VMEM((2 * nh, hd, hd), F32)],
        compiler_params=_cparams(("parallel",)),
        name="deltanet",
    )(qkv, z, gb, gbt, conv_w, norm_g.reshape(1, hd), s0f, s0b)


def _dft_tables(l):
    n = 2 * l
    f = jnp.arange(l, dtype=jnp.int32)
    ft = (f[:, None] * f[None, :]) % n
    ang = ft.astype(F32) * (2.0 * math.pi / n)
    return jnp.cos(ang).astype(BF16), jnp.sin(ang).astype(BF16)


def _hy_features(l):
    pos = jnp.arange(l, dtype=F32)
    t = (pos / max(l - 1, 1))[:, None]
    bands = jnp.linspace(1e-4, HY_BANDS - 1, HY_BANDS, dtype=F32)
    ang = (2.0 * math.pi / l) * pos[:, None] * bands[None, :]
    feats = jnp.concatenate([t, jnp.cos(ang), -jnp.sin(ang)], axis=-1)
    feats = jnp.pad(feats, ((0, 0), (0, LANES - feats.shape[1])))
    deltas = jnp.abs(jnp.linspace(math.log(HY_TARGET) / HY_SLOW, math.log(HY_TARGET) / HY_FAST,
                                  HY_WIDTH, dtype=F32))
    return feats, t, deltas[None, :]


def _hy_filter_kernel(feats_ref, t_ref, deltas_ref, w1_ref, b1_ref, w2_ref, b2_ref, w3_ref, freq_ref,
                      cos_ref, sin_ref, ka_ref, kb_ref, kn_ref):
    l = feats_ref.shape[0]
    n = 2 * l
    freq = freq_ref[...]
    hh = jnp.sin(freq * (_mm_f32(feats_ref[...], w1_ref[...]) + b1_ref[...]))
    hh = jnp.sin(freq * (_mm_f32(hh, w2_ref[...]) + b2_ref[...]))
    hh = _mm_f32(hh, w3_ref[...])
    window = jnp.exp(-t_ref[...] * deltas_ref[...])
    h_fwd = hh[:, :HY_WIDTH] * window
    h_bwd = hh[:, HY_WIDTH:] * window
    row = lax.broadcasted_iota(jnp.int32, (l, HY_WIDTH), 0)
    h_bwd = jnp.where(row == 0, 0.0, h_bwd)
    hs = h_fwd + h_bwd
    hd = h_fwd - h_bwd
    ka_ref[...] = _mm_split(cos_ref[...], hs) * jnp.where(row == 0, 1.0 / n, 2.0 / n)
    kb_ref[...] = _mm_split(sin_ref[...], hd) * (2.0 / n)
    nyq = jnp.sum(jnp.where((row & 1) == 0, hs, -hs), axis=0, keepdims=True) * (1.0 / n)
    kn_ref[...] = jnp.broadcast_to(nyq, kn_ref.shape)


def _hy_filter(l, lp, tables):
    cos_t, sin_t, feats, t, deltas = tables
    w1 = jnp.pad(lp['hy_w1'], ((0, LANES - lp['hy_w1'].shape[0]), (0, 0)))
    row = lambda v: v.reshape(1, -1)
    args = (feats, t, deltas, w1, row(lp['hy_b1']), lp['hy_w2'], row(lp['hy_b2']), lp['hy_w3'],
            row(lp['hy_freq']), cos_t, sin_t)
    return pl.pallas_call(
        _hy_filter_kernel,
        out_shape=[jax.ShapeDtypeStruct((l, HY_WIDTH), F32),
                   jax.ShapeDtypeStruct((l, HY_WIDTH), F32),
                   jax.ShapeDtypeStruct((8, HY_WIDTH), F32)],
        compiler_params=pltpu.CompilerParams(vmem_limit_bytes=VMEM_LIMIT),
        name="hyena_filter",
    )(*args)


def _hyena_kernel(hy_ref, cw_ref, cb_ref, db_ref, cos_ref, sin_ref, ka_ref, kb_ref, kn_ref, o_ref, *, row_len):
    l = hy_ref.shape[1]
    u = _conv3(hy_ref[0].astype(F32), cw_ref[...], row_len) + cb_ref[...]
    x0 = u[:, :HY_WIDTH]
    z = u[:, HY_WIDTH:2 * HY_WIDTH] * u[:, 2 * HY_WIDTH:]
    zb = z.astype(BF16)
    cos_t = cos_ref[...]
    sin_t = sin_ref[...]
    xa = jnp.dot(cos_t, zb, preferred_element_type=F32)
    xb = jnp.dot(sin_t, zb, preferred_element_type=F32)
    ka = ka_ref[...]
    kb = kb_ref[...]
    ya = (xa * ka - xb * kb).astype(BF16)
    yb = (xa * kb + xb * ka).astype(BF16)
    y = jnp.dot(cos_t, ya, preferred_element_type=F32) + jnp.dot(sin_t, yb, preferred_element_type=F32)
    row = lax.broadcasted_iota(jnp.int32, (l, HY_WIDTH), 0)
    even = (row & 1) == 0
    x_nyq = jnp.sum(jnp.where(even, z, -z), axis=0, keepdims=True)
    y_nyq = x_nyq * kn_ref[0:1]
    y = y + jnp.where(even, y_nyq, -y_nyq)
    o_ref[0] = (x0 * (y + z * db_ref[...])).astype(BF16)


def _hyena(hy, lp, filt, tables, row_len):
    b, l, w3 = hy.shape
    ka, kb, kn = filt
    cos_t, sin_t = tables[0], tables[1]
    const = lambda shape: pl.BlockSpec(shape, lambda i: (0,) * len(shape), pipeline_mode=pl.Buffered(1))
    return pl.pallas_call(
        functools.partial(_hyena_kernel, row_len=row_len),
        grid=(b,),
        in_specs=[pl.BlockSpec((1, l, w3), lambda i: (i, 0, 0)),
                  const((3, w3)), const((1, w3)), const((1, HY_WIDTH)),
                  const((l, l)), const((l, l)),
                  const((l, HY_WIDTH)), const((l, HY_WIDTH)), const((8, HY_WIDTH))],
        out_specs=pl.BlockSpec((1, l, HY_WIDTH), lambda i: (i, 0, 0)),
        out_shape=jax.ShapeDtypeStruct((b, l, HY_WIDTH), BF16),
        compiler_params=_cparams(("parallel",)),
        name="hyena",
    )(hy, lp['hy_conv_w'], lp['hy_conv_b'].reshape(1, w3), lp['hy_d_bias'].reshape(1, HY_WIDTH),
      cos_t, sin_t, ka, kb, kn)


def _merge_kernel(x_ref, ya_ref, yb_ref, sc_ref, gates_ref, mod_ref, n2_ref, scw_ref, wa_ref, wb_ref,
                  wc_ref, wo_ref, wr_ref, xo_ref, h2_ref, lg_ref, *, row_len):
    d = x_ref.shape[2]
    sc = sc_ref[0].astype(F32)
    bg = sc[:, :SC_WIDTH]
    yc = bg * _conv3(sc[:, SC_WIDTH:2 * SC_WIDTH] * sc[:, 2 * SC_WIDTH:], scw_ref[...], row_len)
    gate = lambda j: _sigmoid(gates_ref[0, :, j * d:(j + 1) * d].astype(F32))
    merged = (gate(0) * jnp.dot(ya_ref[0], wa_ref[...], preferred_element_type=F32)
              + gate(1) * jnp.dot(yb_ref[0], wb_ref[...], preferred_element_type=F32)
              + gate(2) * _mm(yc, wc_ref[...]))
    m = mod_ref[0]
    x = x_ref[0] + m[2:3] * _mm(merged, wo_ref[...])
    xo_ref[0] = x
    h2 = _rms_mod(x, n2_ref[...], m[4:5], m[3:4])
    h2_ref[0] = h2.astype(BF16)
    lg_ref[0] = _mm_f32(h2, wr_ref[...])


def _merge(x, ya, yb, sc, gates, mod, lp, tm, row_len):
    b, l, d = x.shape
    tok = lambda w: pl.BlockSpec((1, tm, w), lambda i, j: (i, j, 0))
    const = lambda shape: pl.BlockSpec(shape, lambda i, j: (0,) * len(shape), pipeline_mode=pl.Buffered(1))
    wr = jnp.pad(lp['router_w'], ((0, 0), (0, LANES - N_EXPERTS)))
    return pl.pallas_call(
        functools.partial(_merge_kernel, row_len=row_len),
        grid=(b, l // tm),
        in_specs=[tok(d), tok(DN_WIDTH), tok(HY_WIDTH), tok(3 * SC_WIDTH), tok(N_BRANCH * d),
                  pl.BlockSpec((1, 8, d), lambda i, j: (i, 0, 0)),
                  const((1, d)), const((3, SC_WIDTH)),
                  const((DN_WIDTH, d)), const((HY_WIDTH, d)), const((SC_WIDTH, d)), const((d, d)),
                  const((d, LANES))],
        out_specs=[tok(d), tok(d), tok(LANES)],
        out_shape=[jax.ShapeDtypeStruct((b, l, d), F32),
                   jax.ShapeDtypeStruct((b, l, d), BF16),
                   jax.ShapeDtypeStruct((b, l, LANES), F32)],
        compiler_params=_cparams(("parallel", "parallel")),
        name="merge",
    )(x, ya, yb, sc, gates, mod, lp['norm2_g'].reshape(1, d), lp['sc_conv_w'],
      lp['w_branch_a'].astype(BF16), lp['w_branch_b'].astype(BF16), lp['w_branch_c'].astype(BF16),
      lp['w_out'].astype(BF16), wr)


_PREFIX_BLOCK = 256


def _router_kernel(lg_ref, slot_ref, aff_ref, slott_ref, afft_ref, *, cap):
    l = lg_ref.shape[1]
    lane = lax.broadcasted_iota(jnp.int32, (l, LANES), 1)
    valid = lane < N_EXPERTS
    lg = jnp.where(valid, lg_ref[0], -jnp.inf)
    e = jnp.where(valid, jnp.exp(lg - jnp.max(lg, axis=-1, keepdims=True)), 0.0)
    aff = e / jnp.sum(e, axis=-1, keepdims=True)

    def count(mask):
        return jnp.sum(jnp.where(mask, 1.0, 0.0), axis=0, keepdims=True)

    def bisect(i, lo):
        cand = lo | jnp.left_shift(jnp.int32(1), 30 - i)
        return jnp.where(count(aff >= pltpu.bitcast(cand, F32)) >= cap, cand, lo)

    thr = pltpu.bitcast(lax.fori_loop(0, 31, bisect, jnp.zeros((1, LANES), jnp.int32)), F32)
    above = aff > thr
    tie = aff == thr
    need = cap - count(above)

    pb = _PREFIX_BLOCK if l % _PREFIX_BLOCK == 0 else l
    ii = lax.broadcasted_iota(jnp.int32, (pb, pb), 0)
    jj = lax.broadcasted_iota(jnp.int32, (pb, pb), 1)
    before = jnp.where(ii > jj, 1.0, 0.0).astype(BF16)

    def prefix(mask):
        vals = jnp.where(mask, 1.0, 0.0)
        out = []
        offset = jnp.zeros((1, LANES), F32)
        for r in range(l // pb):
            blk = vals[r * pb:(r + 1) * pb]
            out.append(jnp.dot(before, blk.astype(BF16), preferred_element_type=F32) + offset)
            offset = offset + jnp.sum(blk, axis=0, keepdims=True)
        return jnp.concatenate(out, axis=0)

    chosen = jnp.logical_or(above, jnp.logical_and(tie, prefix(tie) < need))
    chosen = jnp.logical_and(chosen, valid)
    slot = jnp.where(chosen, prefix(chosen), -1.0)
    slot_ref[0] = slot
    aff_ref[0] = aff
    slott_ref[0] = slot.T
    afft_ref[0] = aff.T


def _router(logits, cap):
    b, l, _ = logits.shape
    blk = pl.BlockSpec((1, l, LANES), lambda i: (i, 0, 0))
    blk_t = pl.BlockSpec((1, LANES, l), lambda i: (i, 0, 0))
    return pl.pallas_call(
        functools.partial(_router_kernel, cap=cap),
        grid=(b,),
        in_specs=[blk],
        out_specs=[blk, blk, blk_t, blk_t],
        out_shape=[jax.ShapeDtypeStruct((b, l, LANES), F32), jax.ShapeDtypeStruct((b, l, LANES), F32),
                   jax.ShapeDtypeStruct((b, LANES, l), F32), jax.ShapeDtypeStruct((b, LANES, l), F32)],
        compiler_params=_cparams(("parallel",)),
        name="router",
    )(logits)


def _moe_kernel(h_ref, slot_ref, aff_ref, wgu_ref, wd_ref, o_ref, *, cap, ff):
    e = pl.program_id(1)
    l = h_ref.shape[1]
    slot = slot_ref[0, 0]
    want = lax.broadcasted_iota(jnp.int32, (cap, l), 0).astype(F32)
    hit = want == slot
    onehot = jnp.where(hit, 1.0, 0.0).astype(BF16)
    val = jnp.sum(jnp.where(hit, aff_ref[0, 0], 0.0), axis=-1, keepdims=True)
    xe = jnp.dot(onehot, h_ref[0], preferred_element_type=F32).astype(BF16)
    gu = jnp.dot(xe, wgu_ref[0], preferred_element_type=F32)
    hid = (_silu(gu[:, :ff]) * gu[:, ff:]).astype(BF16)
    ye = (jnp.dot(hid, wd_ref[0], preferred_element_type=F32) * val).astype(BF16)
    contrib = lax.dot_general(onehot, ye, (((0,), (0,)), ((), ())), preferred_element_type=F32)

    @pl.when(e == 0)
    def _():
        o_ref[0] = contrib

    @pl.when(e != 0)
    def _():
        o_ref[0] += contrib


def _moe(h2, slot_t, aff_t, wgu, wd, cap):
    b, l, d = h2.shape
    ne, _, ff2 = wgu.shape
    row = lambda a: a[:, :ne].reshape(b, ne, 1, l)
    rspec = pl.BlockSpec((1, 1, 1, l), lambda i, j: (i, j, 0, 0))
    return pl.pallas_call(
        functools.partial(_moe_kernel, cap=cap, ff=ff2 // 2),
        grid=(b, ne),
        in_specs=[pl.BlockSpec((1, l, d), lambda i, j: (i, 0, 0)),
                  rspec, rspec,
                  pl.BlockSpec((1, d, ff2), lambda i, j: (j, 0, 0)),
                  pl.BlockSpec((1, ff2 // 2, d), lambda i, j: (j, 0, 0))],
        out_specs=pl.BlockSpec((1, l, d), lambda i, j: (i, 0, 0)),
        out_shape=jax.ShapeDtypeStruct((b, l, d), F32),
        compiler_params=_cparams(("parallel", "arbitrary")),
        name="experts",
    )(h2, row(slot_t), row(aff_t), wgu, wd)


def _final_kernel(x_ref, moe_ref, mod_ref, g_ref, o_ref):
    x = x_ref[0] + mod_ref[0][5:6] * moe_ref[0]
    o_ref[0] = x * lax.rsqrt(jnp.mean(x * x, axis=-1, keepdims=True) + NORM_EPS) * g_ref[...]


def _final(x, moe, mod, g, tm):
    b, l, d = x.shape
    tok = pl.BlockSpec((1, tm, d), lambda i, j: (i, j, 0))
    return pl.pallas_call(
        _final_kernel,
        grid=(b, l // tm),
        in_specs=[tok, tok, pl.BlockSpec((1, 8, d), lambda i, j: (i, 0, 0)),
                  pl.BlockSpec((1, d), lambda i, j: (0, 0))],
        out_specs=tok,
        out_shape=jax.ShapeDtypeStruct((b, l, d), F32),
        compiler_params=_cparams(("parallel", "parallel")),
        name="final_norm",
    )(x, moe, mod, g.reshape(1, d))


def _token_block(l):
    return min(l, 512)


def kernel(x, c, ctx, c_ctx, ada_w, ada_b, norm1_g, norm2_g, w_in, dn_conv_w, dn_a_log, dn_dt_bias, dn_norm_g, hy_conv_w, hy_conv_b, hy_w1, hy_b1, hy_w2, hy_b2, hy_w3, hy_freq, hy_d_bias, sc_conv_w, w_branch_a, w_branch_b, w_branch_c, w_out, router_w, exp_w_gate, exp_w_up, exp_w_down, final_norm_g):
    depth = ada_w.shape[0]
    b, l, d = x.shape
    lc = ctx.shape[1]
    tm_x, tm_c = _token_block(l), _token_block(lc)
    cap_x = CAPACITY_FACTOR * l // N_EXPERTS
    cap_c = CAPACITY_FACTOR * lc // N_EXPERTS

    rows = -(-(b + 1) // 8) * 8
    cvec = jnp.zeros((rows, d), F32).at[:b].set(c).at[b].set(c_ctx)
    mod_all = _modulation(cvec, ada_w, ada_b).reshape(depth, rows, 6, d)
    mod_all = jnp.pad(mod_all, ((0, 0), (0, 0), (0, 2), (0, 0)))
    mods_x = [mod_all[i, :b] for i in range(depth)]
    mods_c = [jnp.broadcast_to(mod_all[i, b], (b, 8, d)) for i in range(depth)]

    tables_x = _dft_tables(l) + _hy_features(l)
    tables_c = _dft_tables(lc) + _hy_features(lc)
    zero_state = jnp.zeros((b, DN_HEADS, DN_HEAD_DIM, DN_HEAD_DIM), F32)

    moe_x = moe_c = None
    for i in range(depth):
        lp = {
            'dn_conv_w': dn_conv_w[i], 'dn_a_log': dn_a_log[i], 'dn_dt_bias': dn_dt_bias[i],
            'dn_norm_g': dn_norm_g[i], 'hy_conv_w': hy_conv_w[i], 'hy_conv_b': hy_conv_b[i],
            'hy_w1': hy_w1[i], 'hy_b1': hy_b1[i], 'hy_w2': hy_w2[i], 'hy_b2': hy_b2[i], 'hy_w3': hy_w3[i],
            'hy_freq': hy_freq[i], 'hy_d_bias': hy_d_bias[i], 'sc_conv_w': sc_conv_w[i],
            'w_branch_a': w_branch_a[i], 'w_branch_b': w_branch_b[i], 'w_branch_c': w_branch_c[i],
            'w_out': w_out[i], 'router_w': router_w[i], 'norm2_g': norm2_g[i],
        }
        w_packed = _pack_w_in(w_in[i])
        last = i == depth - 1
        wgu = jnp.concatenate([exp_w_gate[i], exp_w_up[i]], axis=-1).astype(BF16)
        wdn = exp_w_down[i].astype(BF16)

        def mixer_inputs(tokens, moe, mods, tm):
            if moe is None:
                outs = _inproj(tokens, mods[i], norm1_g[i], w_packed, tm)
                return tokens, outs
            outs = _inproj(tokens, mods[i], norm1_g[i], w_packed, tm, moe=moe, mod_prev=mods[i - 1])
            return outs[-1], outs[:-1]

        def rest_of_layer(tokens, parts, ya, mods, tm, row_len, tables, cap):
            _, _, hy, sc, gates, _ = parts
            filt = _hy_filter(tokens.shape[1], lp, tables)
            yb = _hyena(hy, lp, filt, tables, row_len)
            tokens, h2, logits = _merge(tokens, ya, yb, sc, gates, mods[i], lp, tm, row_len)
            _, _, slot_t, aff_t = _router(logits, cap)
            return tokens, _moe(h2, slot_t, aff_t, wgu, wdn, cap)

        ctx, parts_c = mixer_inputs(ctx, moe_c, mods_c, tm_c)
        gb_c = _dn_gates(parts_c[5], lp['dn_a_log'], lp['dn_dt_bias'])
        ya_c, s_f, s_b = _deltanet(parts_c[0], parts_c[1], gb_c, lp['dn_conv_w'], lp['dn_norm_g'],
                                   zero_state, zero_state, lc)
        if not last:
            ctx, moe_c = rest_of_layer(ctx, parts_c, ya_c, mods_c, tm_c, lc, tables_c, cap_c)

        x, parts_x = mixer_inputs(x, moe_x, mods_x, tm_x)
        gb_x = _dn_gates(parts_x[5], lp['dn_a_log'], lp['dn_dt_bias'])
        ya_x, _, _ = _deltanet(parts_x[0], parts_x[1], gb_x, lp['dn_conv_w'], lp['dn_norm_g'], s_f, s_b, GRID_W)
        x, moe_x = rest_of_layer(x, parts_x, ya_x, mods_x, tm_x, GRID_W, tables_x, cap_x)

    return _final(x, moe_x, mods_x[depth - 1], final_norm_g, tm_x)
```

```python
import functools
import math

import jax
import jax.numpy as jnp
import numpy as np
from jax import lax
from jax.experimental import pallas as pl
from jax.experimental.pallas import tpu as pltpu

F32 = jnp.float32
BF16 = jnp.bfloat16
HIGHEST = lax.Precision.HIGHEST

GRID_W = 64
DN_HEADS = 4
DN_HEAD_DIM = 128
DN_WIDTH = DN_HEADS * DN_HEAD_DIM
DN_CHUNK = 64
HY_WIDTH = 256
HY_BANDS = 16
HY_TARGET = 1e-2
HY_FAST = 0.3
HY_SLOW = 1.5
SC_WIDTH = 256
N_BRANCH = 3
N_EXPERTS = 16
CAPACITY_FACTOR = 2
NORM_EPS = 1e-6

LANES = 128
VMEM_LIMIT = 56 * 1024 * 1024


def _cparams(sem):
    return pltpu.CompilerParams(dimension_semantics=sem, vmem_limit_bytes=VMEM_LIMIT)


def _mm(a, b):
    return jnp.dot(a.astype(BF16), b.astype(BF16), preferred_element_type=F32)


def _mm_nt(a, b):
    return lax.dot_general(a.astype(BF16), b.astype(BF16), (((1,), (1,)), ((), ())),
                           preferred_element_type=F32)


def _mm_tn(a, b):
    return lax.dot_general(a.astype(BF16), b.astype(BF16), (((0,), (0,)), ((), ())),
                           preferred_element_type=F32)


def _mm_f32(a, b):
    return jnp.dot(a, b, precision=HIGHEST, preferred_element_type=F32)


def _mm_split(a_bf16, b):
    hi = b.astype(BF16)
    lo = (b - hi.astype(F32)).astype(BF16)
    return (jnp.dot(a_bf16, hi, preferred_element_type=F32)
            + jnp.dot(a_bf16, lo, preferred_element_type=F32))


def _sigmoid(x):
    return 0.5 * jnp.tanh(0.5 * x) + 0.5


def _silu(x):
    return x * _sigmoid(x)


def _softplus(x):
    return jnp.maximum(x, 0.0) + jnp.log(1.0 + jnp.exp(-jnp.abs(x)))


def _conv3(x, w, row_len):
    t = x.shape[0]
    pos = lax.broadcasted_iota(jnp.int32, x.shape, 0) & (row_len - 1)
    prev = jnp.where(pos == 0, 0.0, pltpu.roll(x, 1, axis=0))
    nxt = jnp.where(pos == row_len - 1, 0.0, pltpu.roll(x, t - 1, axis=0))
    return prev * w[0:1] + x * w[1:2] + nxt * w[2:3]


def _mod_kernel(c_ref, w_ref, b_ref, o_ref):
    o_ref[0] = _mm_f32(_silu(c_ref[...]), w_ref[0]) + b_ref[0]


def _modulation(cvec, ada_w, ada_b):
    depth, d, n6 = ada_w.shape
    rows = cvec.shape[0]
    tn = 1536
    return pl.pallas_call(
        _mod_kernel,
        grid=(depth, n6 // tn),
        in_specs=[pl.BlockSpec((rows, d), lambda i, j: (0, 0)),
                  pl.BlockSpec((1, d, tn), lambda i, j: (i, 0, j)),
                  pl.BlockSpec((1, 1, tn), lambda i, j: (i, 0, j))],
        out_specs=pl.BlockSpec((1, rows, tn), lambda i, j: (i, 0, j)),
        out_shape=jax.ShapeDtypeStruct((depth, rows, n6), F32),
        compiler_params=_cparams(("parallel", "parallel")),
        name="modulation",
    )(cvec, ada_w, ada_b.reshape(depth, 1, n6))


_QKV = (0, 3 * DN_WIDTH)
_Z = (_QKV[0] + _QKV[1], DN_WIDTH)
_HY = (_Z[0] + _Z[1], 3 * HY_WIDTH)
_SC = (_HY[0] + _HY[1], 3 * SC_WIDTH)
_GATES = (_SC[0] + _SC[1], N_BRANCH * 1024)
_BA = (_GATES[0] + _GATES[1], LANES)
N_PACKED = _BA[0] + _BA[1]


def _pack_w_in(w_in):
    a = 4 * DN_WIDTH
    nba = 4 * DN_HEADS
    pad = jnp.zeros(w_in.shape[:-1] + (LANES - nba,), w_in.dtype)
    return jnp.concatenate([w_in[..., :a], w_in[..., a + nba:], w_in[..., a:a + nba], pad], axis=-1).astype(BF16)


def _rms_mod(x, g, scale, shift):
    y = x * lax.rsqrt(jnp.mean(x * x, axis=-1, keepdims=True) + NORM_EPS)
    return (y * g) * (1.0 + scale) + shift


def _inproj_kernel(*refs, has_moe):
    if has_moe:
        x_ref, moe_ref, modp_ref, mod_ref, g_ref, w_ref = refs[:6]
        outs = refs[6:]
        xo_ref = outs[-1]
        x = x_ref[0] + modp_ref[0][5:6] * moe_ref[0]
        xo_ref[0] = x
    else:
        x_ref, mod_ref, g_ref, w_ref = refs[:4]
        outs = refs[4:]
        x = x_ref[0]
    qkv_ref, z_ref, hy_ref, sc_ref, gates_ref, ba_ref = outs[:6]
    m = mod_ref[0]
    hb = _rms_mod(x, g_ref[...], m[1:2], m[0:1]).astype(BF16)

    def proj(sec, lo=0, width=None):
        a = sec[0] + lo
        wd = sec[1] if width is None else width
        return jnp.dot(hb, w_ref[0, :, a:a + wd], preferred_element_type=F32)

    qkv_ref[0] = proj(_QKV).astype(BF16)
    z_ref[0] = proj(_Z).astype(BF16)
    hy_ref[0] = proj(_HY).astype(BF16)
    sc_ref[0] = proj(_SC).astype(BF16)
    for j in range(N_BRANCH):
        gates_ref[0, :, j * 1024:(j + 1) * 1024] = proj(_GATES, j * 1024, 1024).astype(BF16)
    ba_ref[0] = proj(_BA)


def _inproj(x, mod, norm_g, w_packed, layer, tm, moe=None, mod_prev=None):
    b, l, d = x.shape
    has_moe = moe is not None
    tok = pl.BlockSpec((1, tm, d), lambda i, j: (i, j, 0))
    modspec = pl.BlockSpec((1, 8, d), lambda i, j: (i, 0, 0))
    in_specs = [tok]
    args = [x]
    if has_moe:
        in_specs += [tok, modspec]
        args += [moe, mod_prev]
    in_specs += [modspec,
                 pl.BlockSpec((1, d), lambda i, j: (0, 0)),
                 pl.BlockSpec((1, d, N_PACKED), lambda i, j: (layer, 0, 0), pipeline_mode=pl.Buffered(1))]
    args += [mod, norm_g.reshape(1, d), w_packed]
    widths = [(_QKV[1], BF16), (_Z[1], BF16), (_HY[1], BF16), (_SC[1], BF16), (_GATES[1], BF16), (_BA[1], F32)]
    out_specs = [pl.BlockSpec((1, tm, w), lambda i, j: (i, j, 0)) for w, _ in widths]
    out_shape = [jax.ShapeDtypeStruct((b, l, w), dt) for w, dt in widths]
    if has_moe:
        out_specs.append(tok)
        out_shape.append(jax.ShapeDtypeStruct((b, l, d), F32))
    return pl.pallas_call(
        functools.partial(_inproj_kernel, has_moe=has_moe),
        grid=(b, l // tm),
        in_specs=in_specs,
        out_specs=out_specs,
        out_shape=out_shape,
        compiler_params=_cparams(("parallel", "parallel")),
        name="inproj",
    )(*args)


_BETA_LANE = 0
_DECAY_LANE = 2 * DN_HEADS
_BWD_DECAY_LANE = _DECAY_LANE + DN_HEADS


def _dn_gates_kernel(ba_ref, alog_ref, dtb_ref, o_ref, *, nc):
    x = ba_ref[0]
    beta = _sigmoid(x)
    g = -jnp.exp(alog_ref[...]) * _softplus(x + dtb_ref[...])
    ii = lax.broadcasted_iota(jnp.int32, (DN_CHUNK, DN_CHUNK), 0)
    jj = lax.broadcasted_iota(jnp.int32, (DN_CHUNK, DN_CHUNK), 1)
    lower = jnp.where(ii >= jj, 1.0, 0.0).astype(F32)
    upper = jnp.where(ii <= jj, 1.0, 0.0).astype(F32)
    lane = lax.broadcasted_iota(jnp.int32, (DN_CHUNK, LANES), 1)
    for c in range(nc):
        rows = slice(c * DN_CHUNK, (c + 1) * DN_CHUNK)
        gch = g[rows]
        acc = jnp.where(lane >= _BWD_DECAY_LANE, _mm_f32(upper, gch), _mm_f32(lower, gch))
        o_ref[0, rows, :] = jnp.where(lane < _DECAY_LANE, beta[rows], acc)


def _dn_gates(ba, a_log, dt_bias):
    b, l, _ = ba.shape
    nc = l // DN_CHUNK
    vec = lambda p: jnp.zeros((1, LANES), F32).at[0, _DECAY_LANE:_DECAY_LANE + 2 * DN_HEADS].set(p.reshape(-1))
    return pl.pallas_call(
        functools.partial(_dn_gates_kernel, nc=nc),
        grid=(b,),
        in_specs=[pl.BlockSpec((1, l, LANES), lambda i: (i, 0, 0)),
                  pl.BlockSpec((1, LANES), lambda i: (0, 0)),
                  pl.BlockSpec((1, LANES), lambda i: (0, 0))],
        out_specs=pl.BlockSpec((1, l, LANES), lambda i: (i, 0, 0)),
        out_shape=jax.ShapeDtypeStruct((b, l, LANES), F32),
        compiler_params=_cparams(("parallel",)),
        name="dn_gates",
    )(ba, vec(a_log), vec(dt_bias))


_GROUP_CHUNKS = 4


def _dn_kernel(qkv_ref, z_ref, gb_ref, gbt_ref, cw_ref, ng_ref, s0f_ref, s0b_ref, ya_ref, sf_ref, sb_ref,
               q_s, k_s, v_s, of_s, ob_s, st_s, *, ngroups, gc, row_len):
    c, hd, nh = DN_CHUNK, DN_HEAD_DIM, DN_HEADS
    gt = gc * c
    shift = c.bit_length() - 1

    for part, dst in enumerate((q_s, k_s, v_s)):
        for h in range(nh):
            src = slice((part * nh + h) * hd, (part * nh + h + 1) * hd)
            t = _silu(_conv3(qkv_ref[0, :, src].astype(F32), cw_ref[:, src], row_len))
            if part < 2:
                t = t * lax.rsqrt(jnp.sum(t * t, axis=-1, keepdims=True) + 1e-6)
            if part == 0:
                t = t * (hd ** -0.5)
            dst[:, h * hd:(h + 1) * hd] = t
    for h in range(nh):
        st_s[h] = s0f_ref[0, h]
        st_s[nh + h] = s0b_ref[0, h]

    rr = lax.broadcasted_iota(jnp.int32, (c, gt), 0)
    ln = lax.broadcasted_iota(jnp.int32, (c, gt), 1)
    lblk = ln >> shift
    cc = ln & (c - 1)
    eye_p = jnp.where(rr == cc, 1.0, 0.0).astype(F32)
    inside = [(rr >> lv) == (cc >> lv) for lv in range(1, shift + 1)]
    level_masks = [jnp.where(inside[0], 1.0, 0.0).astype(F32)] + [
        jnp.where(jnp.logical_and(inside[j], jnp.logical_not(inside[j - 1])), 1.0, 0.0).astype(F32)
        for j in range(1, shift)]
    same_blk = ((lax.broadcasted_iota(jnp.int32, (gt, gt), 0) >> shift)
                == (lax.broadcasted_iota(jnp.int32, (gt, gt), 1) >> shift))
    blk_mask = jnp.where(same_blk, 1.0, 0.0).astype(BF16)
    rblk = lax.broadcasted_iota(jnp.int32, (gt, 1), 0) >> shift

    def pack_cols(col):
        out = jnp.broadcast_to(col[0:c], (c, gt))
        for i in range(1, gc):
            out = jnp.where(lblk == i, jnp.broadcast_to(col[i * c:(i + 1) * c], (c, gt)), out)
        return out

    def pack_diag(full):
        out = full[0:c]
        for i in range(1, gc):
            out = jnp.where(lblk == i, full[i * c:(i + 1) * c], out)
        return out

    def block_diag(packed_bf16):
        return jnp.concatenate([packed_bf16] * gc, axis=0) * blk_mask

    def dotf(a, b):
        return jnp.dot(a, b, preferred_element_type=F32)

    def prepare(h, d, g):
        r0 = pl.multiple_of(g * gt, gt)
        hs = slice(h * hd, (h + 1) * hd)
        qg = q_s[pl.ds(r0, gt), hs]
        kg = k_s[pl.ds(r0, gt), hs]
        vg = v_s[pl.ds(r0, gt), hs]
        gbg = gb_ref[0, pl.ds(r0, gt), :]
        col = d * nh + h
        beta = gbg[:, _BETA_LANE + col:_BETA_LANE + col + 1]
        gcc = gbg[:, _DECAY_LANE + col:_DECAY_LANE + col + 1]
        gcr = gbt_ref[0, g][_DECAY_LANE + col:_DECAY_LANE + col + 1, :]
        if d == 0:
            incl, strict, last = rr >= cc, rr > cc, c - 1
        else:
            incl, strict, last = rr <= cc, rr < cc, 0
        kq = _mm_nt(jnp.concatenate([kg, qg], axis=0), kg)
        decay = jnp.where(incl, jnp.exp(jnp.where(incl, pack_cols(gcc) - gcr, 0.0)), 0.0)
        m = jnp.where(strict, pack_diag(kq[:gt]) * pack_cols(beta) * decay, 0.0)
        attn = pack_diag(kq[gt:]) * decay
        eg = jnp.exp(gcc)
        glast = jnp.broadcast_to(gcc[last:last + 1], (gt, 1))
        for i in range(1, gc):
            glast = jnp.where(rblk == i, gcc[i * c + last:i * c + last + 1], glast)
        return dict(m=m, attn=attn.astype(BF16), qd=(qg * eg).astype(BF16),
                    kd=(kg * jnp.exp(glast - gcc)).astype(BF16),
                    rhs=jnp.concatenate([vg * beta, kg * (beta * eg)], axis=1).astype(BF16),
                    chunk_decay=[jnp.exp(gcc[i * c + last:i * c + last + 1]) for i in range(gc)])

    def body(j, carry):
        groups = (j, ngroups - 1 - j)
        slots = [(h, d) for h in range(nh) for d in range(2)]
        state = [st_s[d * nh + h] for h, d in slots]
        pre = [prepare(h, d, groups[d]) for h, d in slots]
        ts = [eye_p - p['m'] * level_masks[0] for p in pre]
        for level_mask in level_masks[1:]:
            xs = [dotf((p['m'] * level_mask).astype(BF16), block_diag(t.astype(BF16))) for p, t in zip(pre, ts)]
            ts = [t - dotf(t.astype(BF16), block_diag(x.astype(BF16))) for t, x in zip(ts, xs)]
        uws = [dotf(block_diag(t.astype(BF16)), p['rhs']) for p, t in zip(pre, ts)]
        outs = [[None] * gc for _ in slots]
        for step in range(gc):
            for n, (h, d) in enumerate(slots):
                i = step if d == 0 else gc - 1 - step
                rows = slice(i * c, (i + 1) * c)
                p, uw, s = pre[n], uws[n], state[n]
                wq = jnp.concatenate([uw[rows, hd:].astype(BF16), p['qd'][rows]], axis=0)
                a = dotf(wq, s.astype(BF16))
                v_new = (uw[rows, :hd] - a[:c]).astype(BF16)
                outs[n][i] = a[c:] + dotf(p['attn'][:, rows], v_new)
                state[n] = s * p['chunk_decay'][i] + lax.dot_general(
                    p['kd'][rows], v_new, (((0,), (0,)), ((), ())), preferred_element_type=F32)
        for n, (h, d) in enumerate(slots):
            st_s[d * nh + h] = state[n]
            r0 = pl.multiple_of(groups[d] * gt, gt)
            (of_s, ob_s)[d][pl.ds(r0, gt), h * hd:(h + 1) * hd] = jnp.concatenate(outs[n], axis=0)
        return carry

    lax.fori_loop(0, ngroups, body, 0)
    for h in range(nh):
        sf_ref[0, h] = st_s[h]
        sb_ref[0, h] = st_s[nh + h]
        hs = slice(h * hd, (h + 1) * hd)
        o = of_s[:, hs] + ob_s[:, hs]
        y = o * lax.rsqrt(jnp.mean(o * o, axis=-1, keepdims=True) + NORM_EPS) * ng_ref[...]
        ya_ref[0, :, hs] = (y * _silu(z_ref[0, :, hs].astype(F32))).astype(BF16)


def _deltanet(qkv, z, gb, conv_w, norm_g, s0f, s0b, row_len):
    b, l, _ = qkv.shape
    c, hd, nh = DN_CHUNK, DN_HEAD_DIM, DN_HEADS
    gc = min(_GROUP_CHUNKS, l // c)
    gt = gc * c
    ngroups = l // gt
    nlanes = 4 * nh
    gbt = jnp.transpose(gb[:, :, :nlanes].reshape(b, ngroups, gt, nlanes), (0, 1, 3, 2))
    st_blk = pl.BlockSpec((1, nh, hd, hd), lambda i: (i, 0, 0, 0))
    seq_blk = lambda w: pl.BlockSpec((1, l, w), lambda i: (i, 0, 0))
    return pl.pallas_call(
        functools.partial(_dn_kernel, ngroups=ngroups, gc=gc, row_len=row_len),
        grid=(b,),
        in_specs=[seq_blk(3 * DN_WIDTH), seq_blk(DN_WIDTH), seq_blk(LANES),
                  pl.BlockSpec((1, ngroups, nlanes, gt), lambda i: (i, 0, 0, 0)),
                  pl.BlockSpec((3, 3 * DN_WIDTH), lambda i: (0, 0)),
                  pl.BlockSpec((1, hd), lambda i: (0, 0)),
                  st_blk, st_blk],
        out_specs=[seq_blk(DN_WIDTH), st_blk, st_blk],
        out_shape=[jax.ShapeDtypeStruct((b, l, DN_WIDTH), BF16),
                   jax.ShapeDtypeStruct((b, nh, hd, hd), F32),
                   jax.ShapeDtypeStruct((b, nh, hd, hd), F32)],
        scratch_shapes=[pltpu.VMEM((l, DN_WIDTH), F32) for _ in range(5)] + [
            pltpu.VMEM((2 * nh, hd, hd), F32)],
        compiler_params=_cparams(("parallel",)),
        name="deltanet",
    )(qkv, z, gb, gbt, conv_w, norm_g.reshape(1, hd), s0f, s0b)


def _dft_tables(l):
    n = 2 * l
    f = jnp.arange(l, dtype=jnp.int32)
    ft = (f[:, None] * f[None, :]) % n
    ang = ft.astype(F32) * (2.0 * math.pi / n)
    return jnp.cos(ang).astype(BF16), jnp.sin(ang).astype(BF16)


def _hy_features(l):
    pos = jnp.arange(l, dtype=F32)
    t = (pos / max(l - 1, 1))[:, None]
    bands = jnp.linspace(1e-4, HY_BANDS - 1, HY_BANDS, dtype=F32)
    ang = (2.0 * math.pi / l) * pos[:, None] * bands[None, :]
    feats = jnp.concatenate([t, jnp.cos(ang), -jnp.sin(ang)], axis=-1)
    feats = jnp.pad(feats, ((0, 0), (0, LANES - feats.shape[1])))
    deltas = jnp.abs(jnp.linspace(math.log(HY_TARGET) / HY_SLOW, math.log(HY_TARGET) / HY_FAST,
                                  HY_WIDTH, dtype=F32))
    return feats, t, deltas[None, :]


def _hy_filter_kernel(feats_ref, t_ref, deltas_ref, w1_ref, b1_ref, w2_ref, b2_ref, w3_ref, freq_ref,
                      cos_ref, sin_ref, ka_ref, kb_ref, kn_ref):
    l = feats_ref.shape[0]
    n = 2 * l
    freq = freq_ref[...]
    hh = jnp.sin(freq * (_mm_f32(feats_ref[...], w1_ref[...]) + b1_ref[...]))
    hh = jnp.sin(freq * (_mm_f32(hh, w2_ref[...]) + b2_ref[...]))
    hh = _mm_f32(hh, w3_ref[...])
    window = jnp.exp(-t_ref[...] * deltas_ref[...])
    h_fwd = hh[:, :HY_WIDTH] * window
    h_bwd = hh[:, HY_WIDTH:] * window
    row = lax.broadcasted_iota(jnp.int32, (l, HY_WIDTH), 0)
    h_bwd = jnp.where(row == 0, 0.0, h_bwd)
    hs = h_fwd + h_bwd
    hd = h_fwd - h_bwd
    ka_ref[...] = _mm_split(cos_ref[...], hs) * jnp.where(row == 0, 1.0 / n, 2.0 / n)
    kb_ref[...] = _mm_split(sin_ref[...], hd) * (2.0 / n)
    nyq = jnp.sum(jnp.where((row & 1) == 0, hs, -hs), axis=0, keepdims=True) * (1.0 / n)
    kn_ref[...] = jnp.broadcast_to(nyq, kn_ref.shape)


def _hy_filter(l, lp, tables):
    cos_t, sin_t, feats, t, deltas = tables
    w1 = jnp.pad(lp['hy_w1'], ((0, LANES - lp['hy_w1'].shape[0]), (0, 0)))
    row = lambda v: v.reshape(1, -1)
    args = (feats, t, deltas, w1, row(lp['hy_b1']), lp['hy_w2'], row(lp['hy_b2']), lp['hy_w3'],
            row(lp['hy_freq']), cos_t, sin_t)
    return pl.pallas_call(
        _hy_filter_kernel,
        out_shape=[jax.ShapeDtypeStruct((l, HY_WIDTH), F32),
                   jax.ShapeDtypeStruct((l, HY_WIDTH), F32),
                   jax.ShapeDtypeStruct((8, HY_WIDTH), F32)],
        compiler_params=pltpu.CompilerParams(vmem_limit_bytes=VMEM_LIMIT),
        name="hyena_filter",
    )(*args)


def _hyena_kernel(hy_ref, cw_ref, cb_ref, db_ref, cos_ref, sin_ref, ka_ref, kb_ref, kn_ref, o_ref, *, row_len):
    l = hy_ref.shape[1]
    u = _conv3(hy_ref[0].astype(F32), cw_ref[...], row_len) + cb_ref[...]
    x0 = u[:, :HY_WIDTH]
    z = u[:, HY_WIDTH:2 * HY_WIDTH] * u[:, 2 * HY_WIDTH:]
    zb = z.astype(BF16)
    cos_t = cos_ref[...]
    sin_t = sin_ref[...]
    xa = jnp.dot(cos_t, zb, preferred_element_type=F32)
    xb = jnp.dot(sin_t, zb, preferred_element_type=F32)
    ka = ka_ref[...]
    kb = kb_ref[...]
    ya = (xa * ka - xb * kb).astype(BF16)
    yb = (xa * kb + xb * ka).astype(BF16)
    y = jnp.dot(cos_t, ya, preferred_element_type=F32) + jnp.dot(sin_t, yb, preferred_element_type=F32)
    row = lax.broadcasted_iota(jnp.int32, (l, HY_WIDTH), 0)
    even = (row & 1) == 0
    x_nyq = jnp.sum(jnp.where(even, z, -z), axis=0, keepdims=True)
    y_nyq = x_nyq * kn_ref[0:1]
    y = y + jnp.where(even, y_nyq, -y_nyq)
    o_ref[0] = (x0 * (y + z * db_ref[...])).astype(BF16)


def _hyena(hy, lp, filt, tables, row_len):
    b, l, w3 = hy.shape
    ka, kb, kn = filt
    cos_t, sin_t = tables[0], tables[1]
    const = lambda shape: pl.BlockSpec(shape, lambda i: (0,) * len(shape), pipeline_mode=pl.Buffered(1))
    return pl.pallas_call(
        functools.partial(_hyena_kernel, row_len=row_len),
        grid=(b,),
        in_specs=[pl.BlockSpec((1, l, w3), lambda i: (i, 0, 0)),
                  const((3, w3)), const((1, w3)), const((1, HY_WIDTH)),
                  const((l, l)), const((l, l)),
                  const((l, HY_WIDTH)), const((l, HY_WIDTH)), const((8, HY_WIDTH))],
        out_specs=pl.BlockSpec((1, l, HY_WIDTH), lambda i: (i, 0, 0)),
        out_shape=jax.ShapeDtypeStruct((b, l, HY_WIDTH), BF16),
        compiler_params=_cparams(("parallel",)),
        name="hyena",
    )(hy, lp['hy_conv_w'], lp['hy_conv_b'].reshape(1, w3), lp['hy_d_bias'].reshape(1, HY_WIDTH),
      cos_t, sin_t, ka, kb, kn)


def _merge_kernel(x_ref, ya_ref, yb_ref, sc_ref, gates_ref, mod_ref, n2_ref, scw_ref, wa_ref, wb_ref,
                  wc_ref, wo_ref, wr_ref, xo_ref, h2_ref, lg_ref, *, row_len):
    d = x_ref.shape[2]
    sc = sc_ref[0].astype(F32)
    bg = sc[:, :SC_WIDTH]
    yc = bg * _conv3(sc[:, SC_WIDTH:2 * SC_WIDTH] * sc[:, 2 * SC_WIDTH:], scw_ref[...], row_len)
    gate = lambda j: _sigmoid(gates_ref[0, :, j * d:(j + 1) * d].astype(F32))
    merged = (gate(0) * jnp.dot(ya_ref[0], wa_ref[0], preferred_element_type=F32)
              + gate(1) * jnp.dot(yb_ref[0], wb_ref[0], preferred_element_type=F32)
              + gate(2) * _mm(yc, wc_ref[0]))
    m = mod_ref[0]
    x = x_ref[0] + m[2:3] * _mm(merged, wo_ref[0])
    xo_ref[0] = x
    h2 = _rms_mod(x, n2_ref[...], m[4:5], m[3:4])
    h2_ref[0] = h2.astype(BF16)
    h_hi = h2.astype(BF16)
    h_lo = (h2 - h_hi.astype(F32)).astype(BF16)
    wr = wr_ref[0]
    w_hi = wr.astype(BF16)
    w_lo = (wr - w_hi.astype(F32)).astype(BF16)
    lg_ref[0] = (jnp.dot(h_hi, w_hi, preferred_element_type=F32) + jnp.dot(h_lo, w_hi, preferred_element_type=F32)
                 + jnp.dot(h_hi, w_lo, preferred_element_type=F32))


def _merge(x, ya, yb, sc, gates, mod, lp, wts, layer, tm, row_len):
    b, l, d = x.shape
    tok = lambda w: pl.BlockSpec((1, tm, w), lambda i, j: (i, j, 0))
    const = lambda shape: pl.BlockSpec(shape, lambda i, j: (0,) * len(shape), pipeline_mode=pl.Buffered(1))
    per_layer = lambda r, c: pl.BlockSpec((1, r, c), lambda i, j: (layer, 0, 0), pipeline_mode=pl.Buffered(1))
    return pl.pallas_call(
        functools.partial(_merge_kernel, row_len=row_len),
        grid=(b, l // tm),
        in_specs=[tok(d), tok(DN_WIDTH), tok(HY_WIDTH), tok(3 * SC_WIDTH), tok(N_BRANCH * d),
                  pl.BlockSpec((1, 8, d), lambda i, j: (i, 0, 0)),
                  const((1, d)), const((3, SC_WIDTH)),
                  per_layer(DN_WIDTH, d), per_layer(HY_WIDTH, d), per_layer(SC_WIDTH, d), per_layer(d, d),
                  per_layer(d, LANES)],
        out_specs=[tok(d), tok(d), tok(LANES)],
        out_shape=[jax.ShapeDtypeStruct((b, l, d), F32),
                   jax.ShapeDtypeStruct((b, l, d), BF16),
                   jax.ShapeDtypeStruct((b, l, LANES), F32)],
        compiler_params=_cparams(("parallel", "parallel")),
        name="merge",
    )(x, ya, yb, sc, gates, mod, lp['norm2_g'].reshape(1, d), lp['sc_conv_w'],
      wts['w_a'], wts['w_b'], wts['w_c'], wts['w_o'], wts['w_r'])


_PREFIX_BLOCK = 256


def _router_kernel(lg_ref, slot_ref, aff_ref, slott_ref, afft_ref, *, cap):
    l = lg_ref.shape[1]
    lane = lax.broadcasted_iota(jnp.int32, (l, LANES), 1)
    valid = lane < N_EXPERTS
    lg = jnp.where(valid, lg_ref[0], -jnp.inf)
    e = jnp.where(valid, jnp.exp(lg - jnp.max(lg, axis=-1, keepdims=True)), 0.0)
    aff = e / jnp.sum(e, axis=-1, keepdims=True)

    def count(mask):
        return jnp.sum(jnp.where(mask, 1.0, 0.0), axis=0, keepdims=True)

    def bisect(i, lo):
        cand = lo | jnp.left_shift(jnp.int32(1), 30 - i)
        return jnp.where(count(aff >= pltpu.bitcast(cand, F32)) >= cap, cand, lo)

    thr = pltpu.bitcast(lax.fori_loop(0, 31, bisect, jnp.zeros((1, LANES), jnp.int32)), F32)
    above = aff > thr
    tie = aff == thr
    need = cap - count(above)

    pb = _PREFIX_BLOCK if l % _PREFIX_BLOCK == 0 else l
    ii = lax.broadcasted_iota(jnp.int32, (pb, pb), 0)
    jj = lax.broadcasted_iota(jnp.int32, (pb, pb), 1)
    before = jnp.where(ii > jj, 1.0, 0.0).astype(BF16)

    def prefix(mask):
        vals = jnp.where(mask, 1.0, 0.0)
        out = []
        offset = jnp.zeros((1, LANES), F32)
        for r in range(l // pb):
            blk = vals[r * pb:(r + 1) * pb]
            out.append(jnp.dot(before, blk.astype(BF16), preferred_element_type=F32) + offset)
            offset = offset + jnp.sum(blk, axis=0, keepdims=True)
        return jnp.concatenate(out, axis=0)

    chosen = jnp.logical_or(above, jnp.logical_and(tie, prefix(tie) < need))
    chosen = jnp.logical_and(chosen, valid)
    slot = jnp.where(chosen, prefix(chosen), -1.0)
    slot_ref[0] = slot
    aff_ref[0] = aff
    slott_ref[0] = slot.T
    afft_ref[0] = aff.T


def _router(logits, cap):
    b, l, _ = logits.shape
    blk = pl.BlockSpec((1, l, LANES), lambda i: (i, 0, 0))
    blk_t = pl.BlockSpec((1, LANES, l), lambda i: (i, 0, 0))
    return pl.pallas_call(
        functools.partial(_router_kernel, cap=cap),
        grid=(b,),
        in_specs=[blk],
        out_specs=[blk, blk, blk_t, blk_t],
        out_shape=[jax.ShapeDtypeStruct((b, l, LANES), F32), jax.ShapeDtypeStruct((b, l, LANES), F32),
                   jax.ShapeDtypeStruct((b, LANES, l), F32), jax.ShapeDtypeStruct((b, LANES, l), F32)],
        compiler_params=_cparams(("parallel",)),
        name="router",
    )(logits)


def _moe_kernel(h_ref, slot_ref, aff_ref, wg_ref, wu_ref, wd_ref, o_ref, *, cap):
    e = pl.program_id(1)
    bs, l, _ = h_ref.shape
    want = lax.broadcasted_iota(jnp.int32, (cap, l), 0).astype(F32)
    onehots, vals, xes = [], [], []
    for s in range(bs):
        hit = want == slot_ref[s, 0]
        onehots.append(jnp.where(hit, 1.0, 0.0).astype(BF16))
        vals.append(jnp.sum(jnp.where(hit, aff_ref[s, 0], 0.0), axis=-1, keepdims=True))
        xes.append(jnp.dot(onehots[s], h_ref[s], preferred_element_type=F32).astype(BF16))
    xe = jnp.concatenate(xes, axis=0)
    hid = (_silu(jnp.dot(xe, wg_ref[0, 0], preferred_element_type=F32))
           * jnp.dot(xe, wu_ref[0, 0], preferred_element_type=F32)).astype(BF16)
    ye = jnp.dot(hid, wd_ref[0, 0], preferred_element_type=F32) * jnp.concatenate(vals, axis=0)
    contribs = [lax.dot_general(onehots[s], ye[s * cap:(s + 1) * cap].astype(BF16), (((0,), (0,)), ((), ())),
                                preferred_element_type=F32) for s in range(bs)]

    @pl.when(e == 0)
    def _():
        for s in range(bs):
            o_ref[s] = contribs[s]

    @pl.when(e != 0)
    def _():
        for s in range(bs):
            o_ref[s] += contribs[s]


def _moe(h2, slot_t, aff_t, wts, layer, cap, bs):
    b, l, d = h2.shape
    _, ne, _, ff = wts['w_gate'].shape
    row = lambda a: a[:, :ne].reshape(b, ne, 1, l)
    rspec = pl.BlockSpec((bs, 1, 1, l), lambda i, j: (i, j, 0, 0))
    tok = pl.BlockSpec((bs, l, d), lambda i, j: (i, 0, 0))
    wspec = lambda r, c: pl.BlockSpec((1, 1, r, c), lambda i, j: (layer, j, 0, 0))
    return pl.pallas_call(
        functools.partial(_moe_kernel, cap=cap),
        grid=(b // bs, ne),
        in_specs=[tok, rspec, rspec, wspec(d, ff), wspec(d, ff), wspec(ff, d)],
        out_specs=tok,
        out_shape=jax.ShapeDtypeStruct((b, l, d), F32),
        compiler_params=_cparams(("parallel", "arbitrary")),
        name="experts",
    )(h2, row(slot_t), row(aff_t), wts['w_gate'], wts['w_up'], wts['w_down'])


def _final_kernel(x_ref, moe_ref, mod_ref, g_ref, o_ref):
    x = x_ref[0] + mod_ref[0][5:6] * moe_ref[0]
    o_ref[0] = x * lax.rsqrt(jnp.mean(x * x, axis=-1, keepdims=True) + NORM_EPS) * g_ref[...]


def _final(x, moe, mod, g, tm):
    b, l, d = x.shape
    tok = pl.BlockSpec((1, tm, d), lambda i, j: (i, j, 0))
    return pl.pallas_call(
        _final_kernel,
        grid=(b, l // tm),
        in_specs=[tok, tok, pl.BlockSpec((1, 8, d), lambda i, j: (i, 0, 0)),
                  pl.BlockSpec((1, d), lambda i, j: (0, 0))],
        out_specs=tok,
        out_shape=jax.ShapeDtypeStruct((b, l, d), F32),
        compiler_params=_cparams(("parallel", "parallel")),
        name="final_norm",
    )(x, moe, mod, g.reshape(1, d))


def _token_block(l):
    return min(l, 512)


_EXPERT_TOKENS_PER_STEP = 2048


def _samples_per_step(b, l):
    bs = max(1, min(b, _EXPERT_TOKENS_PER_STEP // l))
    while b % bs:
        bs -= 1
    return bs


def kernel(x, c, ctx, c_ctx, ada_w, ada_b, norm1_g, norm2_g, w_in, dn_conv_w, dn_a_log, dn_dt_bias, dn_norm_g, hy_conv_w, hy_conv_b, hy_w1, hy_b1, hy_w2, hy_b2, hy_w3, hy_freq, hy_d_bias, sc_conv_w, w_branch_a, w_branch_b, w_branch_c, w_out, router_w, exp_w_gate, exp_w_up, exp_w_down, final_norm_g):
    depth = ada_w.shape[0]
    b, l, d = x.shape
    lc = ctx.shape[1]
    tm_x, tm_c = _token_block(l), _token_block(lc)
    cap_x = CAPACITY_FACTOR * l // N_EXPERTS
    cap_c = CAPACITY_FACTOR * lc // N_EXPERTS

    rows = -(-(b + 1) // 8) * 8
    cvec = jnp.zeros((rows, d), F32).at[:b].set(c).at[b].set(c_ctx)
    mod_all = _modulation(cvec, ada_w, ada_b).reshape(depth, rows, 6, d)
    mod_all = jnp.pad(mod_all, ((0, 0), (0, 0), (0, 2), (0, 0)))
    mods_x = [mod_all[i, :b] for i in range(depth)]
    mods_c = [jnp.broadcast_to(mod_all[i, b], (b, 8, d)) for i in range(depth)]

    tables_x = _dft_tables(l) + _hy_features(l)
    tables_c = _dft_tables(lc) + _hy_features(lc)
    zero_state = jnp.zeros((b, DN_HEADS, DN_HEAD_DIM, DN_HEAD_DIM), F32)

    w_packed = _pack_w_in(w_in)
    wts = {
        'w_a': w_branch_a.astype(BF16), 'w_b': w_branch_b.astype(BF16), 'w_c': w_branch_c.astype(BF16),
        'w_o': w_out.astype(BF16), 'w_r': jnp.pad(router_w, ((0, 0), (0, 0), (0, LANES - N_EXPERTS))),
        'w_gate': exp_w_gate.astype(BF16), 'w_up': exp_w_up.astype(BF16), 'w_down': exp_w_down.astype(BF16),
    }

    moe_x = moe_c = None
    for i in range(depth):
        lp = {
            'dn_conv_w': dn_conv_w[i], 'dn_a_log': dn_a_log[i], 'dn_dt_bias': dn_dt_bias[i],
            'dn_norm_g': dn_norm_g[i], 'hy_conv_w': hy_conv_w[i], 'hy_conv_b': hy_conv_b[i],
            'hy_w1': hy_w1[i], 'hy_b1': hy_b1[i], 'hy_w2': hy_w2[i], 'hy_b2': hy_b2[i], 'hy_w3': hy_w3[i],
            'hy_freq': hy_freq[i], 'hy_d_bias': hy_d_bias[i], 'sc_conv_w': sc_conv_w[i],
            'norm2_g': norm2_g[i],
        }
        last = i == depth - 1

        def mixer_inputs(tokens, moe, mods, tm):
            if moe is None:
                outs = _inproj(tokens, mods[i], norm1_g[i], w_packed, i, tm)
                return tokens, outs
            outs = _inproj(tokens, mods[i], norm1_g[i], w_packed, i, tm, moe=moe, mod_prev=mods[i - 1])
            return outs[-1], outs[:-1]

        def rest_of_layer(tokens, parts, ya, mods, tm, row_len, tables, cap, samples_per_step):
            _, _, hy, sc, gates, _ = parts
            filt = _hy_filter(tokens.shape[1], lp, tables)
            yb = _hyena(hy, lp, filt, tables, row_len)
            tokens, h2, logits = _merge(tokens, ya, yb, sc, gates, mods[i], lp, wts, i, tm, row_len)
            _, _, slot_t, aff_t = _router(logits, cap)
            return tokens, _moe(h2, slot_t, aff_t, wts, i, cap, samples_per_step)

        ctx, parts_c = mixer_inputs(ctx, moe_c, mods_c, tm_c)
        gb_c = _dn_gates(parts_c[5], lp['dn_a_log'], lp['dn_dt_bias'])
        ya_c, s_f, s_b = _deltanet(parts_c[0], parts_c[1], gb_c, lp['dn_conv_w'], lp['dn_norm_g'],
                                   zero_state, zero_state, lc)
        if not last:
            ctx, moe_c = rest_of_layer(ctx, parts_c, ya_c, mods_c, tm_c, lc, tables_c, cap_c, _samples_per_step(b, lc))

        x, parts_x = mixer_inputs(x, moe_x, mods_x, tm_x)
        gb_x = _dn_gates(parts_x[5], lp['dn_a_log'], lp['dn_dt_bias'])
        ya_x, _, _ = _deltanet(parts_x[0], parts_x[1], gb_x, lp['dn_conv_w'], lp['dn_norm_g'], s_f, s_b, GRID_W)
        x, moe_x = rest_of_layer(x, parts_x, ya_x, mods_x, tm_x, GRID_W, tables_x, cap_x, _samples_per_step(b, l))

    return _final(x, moe_x, mods_x[depth - 1], final_norm_g, tm_x)
```

```python
import functools
import math

import jax
import jax.numpy as jnp
import numpy as np
from jax import lax
from jax.experimental import pallas as pl
from jax.experimental.pallas import tpu as pltpu

F32 = jnp.float32
BF16 = jnp.bfloat16
HIGHEST = lax.Precision.HIGHEST

GRID_W = 64
DN_HEADS = 4
DN_HEAD_DIM = 128
DN_WIDTH = DN_HEADS * DN_HEAD_DIM
DN_CHUNK = 64
HY_WIDTH = 256
HY_BANDS = 16
HY_TARGET = 1e-2
HY_FAST = 0.3
HY_SLOW = 1.5
SC_WIDTH = 256
N_BRANCH = 3
N_EXPERTS = 16
CAPACITY_FACTOR = 2
NORM_EPS = 1e-6

LANES = 128
VMEM_LIMIT = 56 * 1024 * 1024


def _cparams(sem):
    return pltpu.CompilerParams(dimension_semantics=sem, vmem_limit_bytes=VMEM_LIMIT)


def _mm(a, b):
    return jnp.dot(a.astype(BF16), b.astype(BF16), preferred_element_type=F32)


def _mm_nt(a, b):
    return lax.dot_general(a.astype(BF16), b.astype(BF16), (((1,), (1,)), ((), ())),
                           preferred_element_type=F32)


def _mm_tn(a, b):
    return lax.dot_general(a.astype(BF16), b.astype(BF16), (((0,), (0,)), ((), ())),
                           preferred_element_type=F32)


def _mm_f32(a, b):
    return jnp.dot(a, b, precision=HIGHEST, preferred_element_type=F32)


def _split_bf16(b):
    hi = b.astype(BF16)
    return hi, (b - hi.astype(F32)).astype(BF16)


def _sigmoid(x):
    return 0.5 * jnp.tanh(0.5 * x) + 0.5


def _silu(x):
    return x * _sigmoid(x)


def _softplus(x):
    return jnp.maximum(x, 0.0) + jnp.log(1.0 + jnp.exp(-jnp.abs(x)))


def _conv3(x, w, row_len):
    t = x.shape[0]
    pos = lax.broadcasted_iota(jnp.int32, x.shape, 0) & (row_len - 1)
    prev = jnp.where(pos == 0, 0.0, pltpu.roll(x, 1, axis=0))
    nxt = jnp.where(pos == row_len - 1, 0.0, pltpu.roll(x, t - 1, axis=0))
    return prev * w[0:1] + x * w[1:2] + nxt * w[2:3]


def _mod_kernel(c_ref, w_ref, b_ref, o_ref):
    o_ref[0] = _mm_f32(_silu(c_ref[...]), w_ref[0]) + b_ref[0]


def _modulation(cvec, ada_w, ada_b):
    depth, d, n6 = ada_w.shape
    rows = cvec.shape[0]
    tn = 1536
    return pl.pallas_call(
        _mod_kernel,
        grid=(depth, n6 // tn),
        in_specs=[pl.BlockSpec((rows, d), lambda i, j: (0, 0)),
                  pl.BlockSpec((1, d, tn), lambda i, j: (i, 0, j)),
                  pl.BlockSpec((1, 1, tn), lambda i, j: (i, 0, j))],
        out_specs=pl.BlockSpec((1, rows, tn), lambda i, j: (i, 0, j)),
        out_shape=jax.ShapeDtypeStruct((depth, rows, n6), F32),
        compiler_params=_cparams(("parallel", "parallel")),
        name="modulation",
    )(cvec, ada_w, ada_b.reshape(depth, 1, n6))


_QKV = (0, 3 * DN_WIDTH)
_Z = (_QKV[0] + _QKV[1], DN_WIDTH)
_HY = (_Z[0] + _Z[1], 3 * HY_WIDTH)
_SC = (_HY[0] + _HY[1], 3 * SC_WIDTH)
_GATES = (_SC[0] + _SC[1], N_BRANCH * 1024)
_BA = (_GATES[0] + _GATES[1], LANES)
N_PACKED = _BA[0] + _BA[1]


def _pack_w_in(w_in):
    a = 4 * DN_WIDTH
    nba = 4 * DN_HEADS
    w = w_in.astype(BF16)
    pad = jnp.zeros(w.shape[:-1] + (LANES - nba,), BF16)
    return jnp.concatenate([w[..., :a], w[..., a + nba:], w[..., a:a + nba], pad], axis=-1)


def _rms_mod(x, g, scale, shift):
    y = x * lax.rsqrt(jnp.mean(x * x, axis=-1, keepdims=True) + NORM_EPS)
    return (y * g) * (1.0 + scale) + shift


def _inproj_kernel(*refs, has_moe, row_len):
    if has_moe:
        x_ref, moe_ref, modp_ref, mod_ref, g_ref, w_ref, cw_ref = refs[:7]
        outs = refs[7:]
        xo_ref = outs[-1]
        x = x_ref[0] + modp_ref[0][5:6] * moe_ref[0]
        xo_ref[0] = x
    else:
        x_ref, mod_ref, g_ref, w_ref, cw_ref = refs[:5]
        outs = refs[5:]
        x = x_ref[0]
    qkv_ref, z_ref, hy_ref, sc_ref, gates_ref, ba_ref = outs[:6]
    m = mod_ref[0]
    hb = _rms_mod(x, g_ref[...], m[1:2], m[0:1]).astype(BF16)

    def proj(sec, lo=0, width=None):
        a = sec[0] + lo
        wd = sec[1] if width is None else width
        return jnp.dot(hb, w_ref[0, :, a:a + wd], preferred_element_type=F32)

    act = _silu(_conv3(proj(_QKV), cw_ref[0], row_len))
    hd = DN_HEAD_DIM
    for j in range(3 * DN_HEADS):
        t = act[:, j * hd:(j + 1) * hd]
        if j < 2 * DN_HEADS:
            t = t * lax.rsqrt(jnp.sum(t * t, axis=-1, keepdims=True) + 1e-6)
        if j < DN_HEADS:
            t = t * (hd ** -0.5)
        qkv_ref[0, :, j * hd:(j + 1) * hd] = t.astype(BF16)
    z_ref[0] = proj(_Z).astype(BF16)
    hy_ref[0] = proj(_HY).astype(BF16)
    sc_ref[0] = proj(_SC).astype(BF16)
    for j in range(N_BRANCH):
        gates_ref[0, :, j * 1024:(j + 1) * 1024] = proj(_GATES, j * 1024, 1024).astype(BF16)
    ba_ref[0] = proj(_BA)


def _inproj(x, mod, norm_g, w_packed, dn_conv_w, layer, tm, row_len, moe=None, mod_prev=None):
    b, l, d = x.shape
    assert tm % row_len == 0
    has_moe = moe is not None
    tok = pl.BlockSpec((1, tm, d), lambda i, j: (i, j, 0))
    modspec = pl.BlockSpec((1, 8, d), lambda i, j: (i, 0, 0))
    in_specs = [tok]
    args = [x]
    if has_moe:
        in_specs += [tok, modspec]
        args += [moe, mod_prev]
    in_specs += [modspec,
                 pl.BlockSpec((1, d), lambda i, j: (0, 0)),
                 pl.BlockSpec((1, d, N_PACKED), lambda i, j: (layer, 0, 0), pipeline_mode=pl.Buffered(1)),
                 pl.BlockSpec((1,) + dn_conv_w.shape[1:], lambda i, j: (layer, 0, 0))]
    args += [mod, norm_g.reshape(1, d), w_packed, dn_conv_w]
    widths = [(_QKV[1], BF16), (_Z[1], BF16), (_HY[1], BF16), (_SC[1], BF16), (_GATES[1], BF16), (_BA[1], F32)]
    out_specs = [pl.BlockSpec((1, tm, w), lambda i, j: (i, j, 0)) for w, _ in widths]
    out_shape = [jax.ShapeDtypeStruct((b, l, w), dt) for w, dt in widths]
    if has_moe:
        out_specs.append(tok)
        out_shape.append(jax.ShapeDtypeStruct((b, l, d), F32))
    return pl.pallas_call(
        functools.partial(_inproj_kernel, has_moe=has_moe, row_len=row_len),
        grid=(b, l // tm),
        in_specs=in_specs,
        out_specs=out_specs,
        out_shape=out_shape,
        compiler_params=_cparams(("parallel", "parallel")),
        name="inproj",
    )(*args)


_BETA_LANE = 0
_DECAY_LANE = 2 * DN_HEADS
_BWD_DECAY_LANE = _DECAY_LANE + DN_HEADS


def _dn_gates_kernel(ba_ref, alog_ref, dtb_ref, o_ref, *, nc):
    x = ba_ref[0]
    beta = _sigmoid(x)
    g = -jnp.exp(alog_ref[...]) * _softplus(x + dtb_ref[...])
    ii = lax.broadcasted_iota(jnp.int32, (DN_CHUNK, DN_CHUNK), 0)
    jj = lax.broadcasted_iota(jnp.int32, (DN_CHUNK, DN_CHUNK), 1)
    lower = jnp.where(ii >= jj, 1.0, 0.0).astype(F32)
    upper = jnp.where(ii <= jj, 1.0, 0.0).astype(F32)
    lane = lax.broadcasted_iota(jnp.int32, (DN_CHUNK, LANES), 1)
    for c in range(nc):
        rows = slice(c * DN_CHUNK, (c + 1) * DN_CHUNK)
        gch = g[rows]
        acc = jnp.where(lane >= _BWD_DECAY_LANE, _mm_f32(upper, gch), _mm_f32(lower, gch))
        o_ref[0, rows, :] = jnp.where(lane < _DECAY_LANE, beta[rows], acc)


def _dn_gates(ba, a_log, dt_bias):
    b, l, _ = ba.shape
    nc = l // DN_CHUNK
    vec = lambda p: jnp.zeros((1, LANES), F32).at[0, _DECAY_LANE:_DECAY_LANE + 2 * DN_HEADS].set(p.reshape(-1))
    return pl.pallas_call(
        functools.partial(_dn_gates_kernel, nc=nc),
        grid=(b,),
        in_specs=[pl.BlockSpec((1, l, LANES), lambda i: (i, 0, 0)),
                  pl.BlockSpec((1, LANES), lambda i: (0, 0)),
                  pl.BlockSpec((1, LANES), lambda i: (0, 0))],
        out_specs=pl.BlockSpec((1, l, LANES), lambda i: (i, 0, 0)),
        out_shape=jax.ShapeDtypeStruct((b, l, LANES), F32),
        compiler_params=_cparams(("parallel",)),
        name="dn_gates",
    )(ba, vec(a_log), vec(dt_bias))


_GROUP_CHUNKS = 4


def _dn_kernel(qkv_ref, z_ref, gb_ref, gbt_ref, ng_ref, s0f_ref, s0b_ref, ya_ref, sf_ref, sb_ref,
               of_s, ob_s, st_s, *, ngroups, gc):
    c, hd, nh = DN_CHUNK, DN_HEAD_DIM, DN_HEADS
    gt = gc * c
    shift = c.bit_length() - 1

    for h in range(nh):
        st_s[h] = s0f_ref[0, h]
        st_s[nh + h] = s0b_ref[0, h]

    rr = lax.broadcasted_iota(jnp.int32, (c, gt), 0)
    ln = lax.broadcasted_iota(jnp.int32, (c, gt), 1)
    lblk = ln >> shift
    cc = ln & (c - 1)
    eye_p = jnp.where(rr == cc, 1.0, 0.0).astype(F32)
    inside = [(rr >> lv) == (cc >> lv) for lv in range(1, shift + 1)]
    level_masks = [jnp.where(inside[0], 1.0, 0.0).astype(F32)] + [
        jnp.where(jnp.logical_and(inside[j], jnp.logical_not(inside[j - 1])), 1.0, 0.0).astype(F32)
        for j in range(1, shift)]
    same_blk = ((lax.broadcasted_iota(jnp.int32, (gt, gt), 0) >> shift)
                == (lax.broadcasted_iota(jnp.int32, (gt, gt), 1) >> shift))
    blk_mask = jnp.where(same_blk, 1.0, 0.0).astype(BF16)
    rblk = lax.broadcasted_iota(jnp.int32, (gt, 1), 0) >> shift

    def pack_cols(col):
        out = jnp.broadcast_to(col[0:c], (c, gt))
        for i in range(1, gc):
            out = jnp.where(lblk == i, jnp.broadcast_to(col[i * c:(i + 1) * c], (c, gt)), out)
        return out

    def pack_diag(full):
        out = full[0:c]
        for i in range(1, gc):
            out = jnp.where(lblk == i, full[i * c:(i + 1) * c], out)
        return out

    def block_diag(packed_bf16):
        return jnp.concatenate([packed_bf16] * gc, axis=0) * blk_mask

    pair_lane = lax.broadcasted_iota(jnp.int32, (1, 2 * hd), 1)
    first_head = jnp.where(pair_lane < hd, 1.0, 0.0).astype(BF16)
    second_head = jnp.where(pair_lane >= hd, 1.0, 0.0).astype(BF16)

    def pair_diag(pair_bf16):
        return jnp.concatenate([pair_bf16 * first_head, pair_bf16 * second_head], axis=0)

    def dotf(a, b):
        return jnp.dot(a, b, preferred_element_type=F32)

    def prepare(h, d, g):
        r0 = pl.multiple_of(g * gt, gt)
        part = lambda p: qkv_ref[0, pl.ds(r0, gt), (p * nh + h) * hd:(p * nh + h + 1) * hd].astype(F32)
        qg, kg, vg = part(0), part(1), part(2)
        gbg = gb_ref[0, pl.ds(r0, gt), :]
        col = d * nh + h
        beta = gbg[:, _BETA_LANE + col:_BETA_LANE + col + 1]
        gcc = gbg[:, _DECAY_LANE + col:_DECAY_LANE + col + 1]
        gcr = gbt_ref[0, g][_DECAY_LANE + col:_DECAY_LANE + col + 1, :]
        if d == 0:
            incl, strict, last = rr >= cc, rr > cc, c - 1
        else:
            incl, strict, last = rr <= cc, rr < cc, 0
        kq = _mm_nt(jnp.concatenate([kg, qg], axis=0), kg)
        decay = jnp.where(incl, jnp.exp(jnp.where(incl, pack_cols(gcc) - gcr, 0.0)), 0.0)
        m = jnp.where(strict, pack_diag(kq[:gt]) * pack_cols(beta) * decay, 0.0)
        attn = pack_diag(kq[gt:]) * decay
        eg = jnp.exp(gcc)
        glast = jnp.broadcast_to(gcc[last:last + 1], (gt, 1))
        for i in range(1, gc):
            glast = jnp.where(rblk == i, gcc[i * c + last:i * c + last + 1], glast)
        return dict(m=m, attn=attn.astype(BF16), qd=(qg * eg).astype(BF16),
                    kd=(kg * jnp.exp(glast - gcc)).astype(BF16),
                    rhs=jnp.concatenate([vg * beta, kg * (beta * eg)], axis=1).astype(BF16),
                    chunk_decay=[jnp.exp(gcc[i * c + last:i * c + last + 1]) for i in range(gc)])

    def body(j, carry):
        groups = (j, ngroups - 1 - j)
        slots = [(h, d) for h in range(nh) for d in range(2)]
        state = [st_s[d * nh + h] for h, d in slots]
        pre = [prepare(h, d, groups[d]) for h, d in slots]
        ts = [eye_p - p['m'] * level_masks[0] for p in pre]
        for level_mask in level_masks[1:]:
            xs = [dotf((p['m'] * level_mask).astype(BF16), block_diag(t.astype(BF16))) for p, t in zip(pre, ts)]
            ts = [t - dotf(t.astype(BF16), block_diag(x.astype(BF16))) for t, x in zip(ts, xs)]
        uws = [dotf(block_diag(t.astype(BF16)), p['rhs']) for p, t in zip(pre, ts)]
        lanes = lambda parts: jnp.concatenate(parts, axis=1)
        index = {slot: n for n, slot in enumerate(slots)}
        pairs = [(hp, d) for hp in range(nh // 2) for d in range(2)]
        packed = []
        for hp, d in pairs:
            na, nb = index[(2 * hp, d)], index[(2 * hp + 1, d)]
            packed.append(dict(
                s=lanes([state[na], state[nb]]),
                u=lanes([uws[na][:, :hd], uws[nb][:, :hd]]),
                w=lanes([uws[na][:, hd:], uws[nb][:, hd:]]).astype(BF16),
                qd=lanes([pre[na]['qd'], pre[nb]['qd']]), a=pre[na], b=pre[nb]))
        outs = [[None] * gc for _ in pairs]
        for step in range(gc):
            for n, (hp, d) in enumerate(pairs):
                i = step if d == 0 else gc - 1 - step
                rows = slice(i * c, (i + 1) * c)
                q = packed[n]
                wq = jnp.concatenate([q['w'][rows], q['qd'][rows]], axis=0)
                a = dotf(wq, pair_diag(q['s'].astype(BF16)))
                v_new = (q['u'][rows] - a[:c]).astype(BF16)
                lhs = jnp.concatenate([lanes([q['a']['attn'][:, rows], q['b']['attn'][:, rows]]),
                                       lanes([q['a']['kd'][rows].T, q['b']['kd'][rows].T])], axis=0)
                av = dotf(lhs, pair_diag(v_new))
                outs[n][i] = a[c:] + av[:c]
                decay = lanes([jnp.broadcast_to(q['a']['chunk_decay'][i], (1, hd)),
                               jnp.broadcast_to(q['b']['chunk_decay'][i], (1, hd))])
                q['s'] = q['s'] * decay + av[c:]
        for n, (hp, d) in enumerate(pairs):
            st_s[d * nh + 2 * hp] = packed[n]['s'][:, :hd]
            st_s[d * nh + 2 * hp + 1] = packed[n]['s'][:, hd:]
            r0 = pl.multiple_of(groups[d] * gt, gt)
            (of_s, ob_s)[d][pl.ds(r0, gt), 2 * hp * hd:(2 * hp + 2) * hd] = jnp.concatenate(outs[n], axis=0)
        return carry

    lax.fori_loop(0, ngroups, body, 0)
    for h in range(nh):
        sf_ref[0, h] = st_s[h]
        sb_ref[0, h] = st_s[nh + h]
        hs = slice(h * hd, (h + 1) * hd)
        o = of_s[:, hs] + ob_s[:, hs]
        y = o * lax.rsqrt(jnp.mean(o * o, axis=-1, keepdims=True) + NORM_EPS) * ng_ref[...]
        ya_ref[0, :, hs] = (y * _silu(z_ref[0, :, hs].astype(F32))).astype(BF16)


def _deltanet(qkv, z, gb, norm_g, s0f, s0b):
    b, l, _ = qkv.shape
    c, hd, nh = DN_CHUNK, DN_HEAD_DIM, DN_HEADS
    gc = min(_GROUP_CHUNKS, l // c)
    gt = gc * c
    ngroups = l // gt
    nlanes = 4 * nh
    gbt = jnp.transpose(gb[:, :, :nlanes].reshape(b, ngroups, gt, nlanes), (0, 1, 3, 2))
    st_blk = pl.BlockSpec((1, nh, hd, hd), lambda i: (i, 0, 0, 0))
    seq_blk = lambda w: pl.BlockSpec((1, l, w), lambda i: (i, 0, 0))
    return pl.pallas_call(
        functools.partial(_dn_kernel, ngroups=ngroups, gc=gc),
        grid=(b,),
        in_specs=[seq_blk(3 * DN_WIDTH), seq_blk(DN_WIDTH), seq_blk(LANES),
                  pl.BlockSpec((1, ngroups, nlanes, gt), lambda i: (i, 0, 0, 0)),
                  pl.BlockSpec((1, hd), lambda i: (0, 0)),
                  st_blk, st_blk],
        out_specs=[seq_blk(DN_WIDTH), st_blk, st_blk],
        out_shape=[jax.ShapeDtypeStruct((b, l, DN_WIDTH), BF16),
                   jax.ShapeDtypeStruct((b, nh, hd, hd), F32),
                   jax.ShapeDtypeStruct((b, nh, hd, hd), F32)],
        scratch_shapes=[pltpu.VMEM((l, DN_WIDTH), F32),
                        pltpu.VMEM((l, DN_WIDTH), F32),
                        pltpu.VMEM((2 * nh, hd, hd), F32)],
        compiler_params=_cparams(("parallel",)),
        name="deltanet",
    )(qkv, z, gb, gbt, norm_g.reshape(1, hd), s0f, s0b)


def _dft_tables(l):
    n = 2 * l
    nb = GRID_W
    f = jnp.arange(l, dtype=jnp.int32)[:, None]
    angle = lambda k: (k % n).astype(F32) * (2.0 * math.pi / n)
    ang_a = angle(f * (nb * jnp.arange(l // nb, dtype=jnp.int32))[None, :])[:, :, None]
    ang_b = angle(f * jnp.arange(nb, dtype=jnp.int32)[None, :])[:, None, :]
    ca, sa, cb, sb = jnp.cos(ang_a), jnp.sin(ang_a), jnp.cos(ang_b), jnp.sin(ang_b)
    return ((ca * cb - sa * sb).reshape(l, l).astype(BF16), (sa * cb + ca * sb).reshape(l, l).astype(BF16))


def _hy_features(l):
    pos = jnp.arange(l, dtype=F32)
    t = (pos / max(l - 1, 1))[:, None]
    bands = jnp.linspace(1e-4, HY_BANDS - 1, HY_BANDS, dtype=F32)
    ang = (2.0 * math.pi / l) * pos[:, None] * bands[None, :]
    feats = jnp.concatenate([t, jnp.cos(ang), -jnp.sin(ang)], axis=-1)
    feats = jnp.pad(feats, ((0, 0), (0, LANES - feats.shape[1])))
    deltas = jnp.abs(jnp.linspace(math.log(HY_TARGET) / HY_SLOW, math.log(HY_TARGET) / HY_FAST,
                                  HY_WIDTH, dtype=F32))
    return feats, t, deltas[None, :]


def _hy_filter_kernel(feats_ref, t_ref, deltas_ref, w1_ref, b1_ref, w2_ref, b2_ref, w3_ref, freq_ref,
                      cos_ref, sin_ref, ka_ref, kb_ref, kn_ref):
    l = feats_ref.shape[0]
    n = 2 * l
    freq = freq_ref[...]
    hh = jnp.sin(freq * (_mm_f32(feats_ref[...], w1_ref[...]) + b1_ref[...]))
    hh = jnp.sin(freq * (_mm_f32(hh, w2_ref[...]) + b2_ref[...]))
    hh = _mm_f32(hh, w3_ref[...])
    window = jnp.exp(-t_ref[...] * deltas_ref[...])
    h_fwd = hh[:, :HY_WIDTH] * window
    h_bwd = hh[:, HY_WIDTH:] * window
    row = lax.broadcasted_iota(jnp.int32, (l, HY_WIDTH), 0)
    h_bwd = jnp.where(row == 0, 0.0, h_bwd)
    hs = h_fwd + h_bwd
    hd = h_fwd - h_bwd
    hs_hi, hs_lo = _split_bf16(hs)
    hd_hi, hd_lo = _split_bf16(hd)
    blk = min(_DFT_ROW_BLOCK, l)
    brow = lax.broadcasted_iota(jnp.int32, (blk, HY_WIDTH), 0)
    for i in range(l // blk):
        rows = slice(i * blk, (i + 1) * blk)
        cos_b = cos_ref[rows, :]
        sin_b = sin_ref[rows, :]
        ka = jnp.dot(cos_b, hs_hi, preferred_element_type=F32) + jnp.dot(cos_b, hs_lo, preferred_element_type=F32)
        kb = jnp.dot(sin_b, hd_hi, preferred_element_type=F32) + jnp.dot(sin_b, hd_lo, preferred_element_type=F32)
        ka_ref[rows, :] = ka * (jnp.where(brow == 0, 1.0 / n, 2.0 / n) if i == 0 else 2.0 / n)
        kb_ref[rows, :] = kb * (2.0 / n)
    nyq = jnp.sum(jnp.where((row & 1) == 0, hs, -hs), axis=0, keepdims=True) * (1.0 / n)
    kn_ref[...] = jnp.broadcast_to(nyq, kn_ref.shape)


def _hy_filter(l, lp, tables):
    cos_t, sin_t, feats, t, deltas = tables
    w1 = jnp.pad(lp['hy_w1'], ((0, LANES - lp['hy_w1'].shape[0]), (0, 0)))
    row = lambda v: v.reshape(1, -1)
    args = (feats, t, deltas, w1, row(lp['hy_b1']), lp['hy_w2'], row(lp['hy_b2']), lp['hy_w3'],
            row(lp['hy_freq']), cos_t, sin_t)
    return pl.pallas_call(
        _hy_filter_kernel,
        out_shape=[jax.ShapeDtypeStruct((l, HY_WIDTH), F32),
                   jax.ShapeDtypeStruct((l, HY_WIDTH), F32),
                   jax.ShapeDtypeStruct((8, HY_WIDTH), F32)],
        compiler_params=pltpu.CompilerParams(vmem_limit_bytes=VMEM_LIMIT),
        name="hyena_filter",
    )(*args)


_DFT_ROW_BLOCK = 256


def _hyena_kernel(hy_ref, cw_ref, cb_ref, db_ref, cos_ref, sin_ref, ka_ref, kb_ref, kn_ref, o_ref, *, row_len):
    l = hy_ref.shape[1]
    u = _conv3(hy_ref[0].astype(F32), cw_ref[...], row_len) + cb_ref[...]
    x0 = u[:, :HY_WIDTH]
    z = u[:, HY_WIDTH:2 * HY_WIDTH] * u[:, 2 * HY_WIDTH:]
    zb = z.astype(BF16)
    blk = min(_DFT_ROW_BLOCK, l)
    blocks = [slice(i * blk, (i + 1) * blk) for i in range(l // blk)]
    ya, yb = [], []
    for rows in blocks:
        xa = jnp.dot(cos_ref[rows, :], zb, preferred_element_type=F32)
        xb = jnp.dot(sin_ref[rows, :], zb, preferred_element_type=F32)
        ka = ka_ref[rows, :]
        kb = kb_ref[rows, :]
        ya.append((xa * ka - xb * kb).astype(BF16))
        yb.append((xa * kb + xb * ka).astype(BF16))
    ya = jnp.concatenate(ya, axis=0)
    yb = jnp.concatenate(yb, axis=0)
    even = (lax.broadcasted_iota(jnp.int32, (blk, HY_WIDTH), 0) & 1) == 0
    sign = jnp.where((lax.broadcasted_iota(jnp.int32, (l, HY_WIDTH), 0) & 1) == 0, 1.0, -1.0)
    y_nyq = jnp.sum(z * sign, axis=0, keepdims=True) * kn_ref[0:1]
    for rows in blocks:
        y = (jnp.dot(cos_ref[rows, :], ya, preferred_element_type=F32)
             + jnp.dot(sin_ref[rows, :], yb, preferred_element_type=F32))
        y = y + jnp.where(even, y_nyq, -y_nyq)
        o_ref[0, rows, :] = (x0[rows] * (y + z[rows] * db_ref[...])).astype(BF16)


def _hyena(hy, lp, filt, tables, row_len):
    b, l, w3 = hy.shape
    ka, kb, kn = filt
    cos_t, sin_t = tables[0], tables[1]
    const = lambda shape: pl.BlockSpec(shape, lambda i: (0,) * len(shape), pipeline_mode=pl.Buffered(1))
    return pl.pallas_call(
        functools.partial(_hyena_kernel, row_len=row_len),
        grid=(b,),
        in_specs=[pl.BlockSpec((1, l, w3), lambda i: (i, 0, 0)),
                  const((3, w3)), const((1, w3)), const((1, HY_WIDTH)),
                  const((l, l)), const((l, l)),
                  const((l, HY_WIDTH)), const((l, HY_WIDTH)), const((8, HY_WIDTH))],
        out_specs=pl.BlockSpec((1, l, HY_WIDTH), lambda i: (i, 0, 0)),
        out_shape=jax.ShapeDtypeStruct((b, l, HY_WIDTH), BF16),
        compiler_params=_cparams(("parallel",)),
        name="hyena",
    )(hy, lp['hy_conv_w'], lp['hy_conv_b'].reshape(1, w3), lp['hy_d_bias'].reshape(1, HY_WIDTH),
      cos_t, sin_t, ka, kb, kn)


def _merge_kernel(x_ref, ya_ref, yb_ref, sc_ref, gates_ref, mod_ref, n2_ref, scw_ref, wa_ref, wb_ref,
                  wc_ref, wo_ref, wr_ref, xo_ref, h2_ref, lg_ref, *, row_len):
    d = x_ref.shape[2]
    sc = sc_ref[0].astype(F32)
    bg = sc[:, :SC_WIDTH]
    yc = bg * _conv3(sc[:, SC_WIDTH:2 * SC_WIDTH] * sc[:, 2 * SC_WIDTH:], scw_ref[...], row_len)
    gate = lambda j: _sigmoid(gates_ref[0, :, j * d:(j + 1) * d].astype(F32))
    merged = (gate(0) * jnp.dot(ya_ref[0], wa_ref[0], preferred_element_type=F32)
              + gate(1) * jnp.dot(yb_ref[0], wb_ref[0], preferred_element_type=F32)
              + gate(2) * _mm(yc, wc_ref[0]))
    m = mod_ref[0]
    x = x_ref[0] + m[2:3] * _mm(merged, wo_ref[0])
    xo_ref[0] = x
    h2 = _rms_mod(x, n2_ref[...], m[4:5], m[3:4])
    h2_ref[0] = h2.astype(BF16)
    h_hi = h2.astype(BF16)
    h_lo = (h2 - h_hi.astype(F32)).astype(BF16)
    wr = wr_ref[0]
    w_hi = wr.astype(BF16)
    w_lo = (wr - w_hi.astype(F32)).astype(BF16)
    lg_ref[0] = (jnp.dot(h_hi, w_hi, preferred_element_type=F32) + jnp.dot(h_lo, w_hi, preferred_element_type=F32)
                 + jnp.dot(h_hi, w_lo, preferred_element_type=F32))


def _merge(x, ya, yb, sc, gates, mod, lp, wts, layer, tm, row_len):
    b, l, d = x.shape
    tok = lambda w: pl.BlockSpec((1, tm, w), lambda i, j: (i, j, 0))
    const = lambda shape: pl.BlockSpec(shape, lambda i, j: (0,) * len(shape), pipeline_mode=pl.Buffered(1))
    per_layer = lambda r, c: pl.BlockSpec((1, r, c), lambda i, j: (layer, 0, 0), pipeline_mode=pl.Buffered(1))
    return pl.pallas_call(
        functools.partial(_merge_kernel, row_len=row_len),
        grid=(b, l // tm),
        in_specs=[tok(d), tok(DN_WIDTH), tok(HY_WIDTH), tok(3 * SC_WIDTH), tok(N_BRANCH * d),
                  pl.BlockSpec((1, 8, d), lambda i, j: (i, 0, 0)),
                  const((1, d)), const((3, SC_WIDTH)),
                  per_layer(DN_WIDTH, d), per_layer(HY_WIDTH, d), per_layer(SC_WIDTH, d), per_layer(d, d),
                  per_layer(d, LANES)],
        out_specs=[tok(d), tok(d), tok(LANES)],
        out_shape=[jax.ShapeDtypeStruct((b, l, d), F32),
                   jax.ShapeDtypeStruct((b, l, d), BF16),
                   jax.ShapeDtypeStruct((b, l, LANES), F32)],
        compiler_params=_cparams(("parallel", "parallel")),
        name="merge",
    )(x, ya, yb, sc, gates, mod, lp['norm2_g'].reshape(1, d), lp['sc_conv_w'],
      wts['w_a'], wts['w_b'], wts['w_c'], wts['w_o'], wts['w_r'])


_PREFIX_BLOCK = 256


def _router_kernel(lg_ref, slot_ref, aff_ref, slott_ref, afft_ref, *, cap):
    l = lg_ref.shape[1]
    lane = lax.broadcasted_iota(jnp.int32, (l, LANES), 1)
    valid = lane < N_EXPERTS
    lg = jnp.where(valid, lg_ref[0], -jnp.inf)
    e = jnp.where(valid, jnp.exp(lg - jnp.max(lg, axis=-1, keepdims=True)), 0.0)
    aff = e / jnp.sum(e, axis=-1, keepdims=True)

    def count(mask):
        return jnp.sum(jnp.where(mask, 1.0, 0.0), axis=0, keepdims=True)

    aff_t = aff.T
    dense = aff_t[:N_EXPERTS]

    def bisect(i, lo):
        cand = lo | jnp.left_shift(jnp.int32(1), 30 - i)
        n_ge = jnp.sum(jnp.where(dense >= pltpu.bitcast(cand, F32)[:, :1], 1.0, 0.0), axis=-1, keepdims=True)
        return jnp.where(n_ge >= cap, cand, lo)

    thr_t = pltpu.bitcast(lax.fori_loop(0, 31, bisect, jnp.zeros((N_EXPERTS, LANES), jnp.int32)), F32)
    on_diag = (lax.broadcasted_iota(jnp.int32, (N_EXPERTS, LANES), 0)
               == lax.broadcasted_iota(jnp.int32, (N_EXPERTS, LANES), 1))
    thr = jnp.sum(jnp.where(on_diag, thr_t, 0.0), axis=0, keepdims=True)
    above = aff > thr
    tie = aff == thr
    need = cap - count(above)

    pb = _PREFIX_BLOCK if l % _PREFIX_BLOCK == 0 else l
    ii = lax.broadcasted_iota(jnp.int32, (pb, pb), 0)
    jj = lax.broadcasted_iota(jnp.int32, (pb, pb), 1)
    before = jnp.where(ii > jj, 1.0, 0.0).astype(BF16)

    def prefix(mask):
        vals = jnp.where(mask, 1.0, 0.0)
        out = []
        offset = jnp.zeros((1, LANES), F32)
        for r in range(l // pb):
            blk = vals[r * pb:(r + 1) * pb]
            out.append(jnp.dot(before, blk.astype(BF16), preferred_element_type=F32) + offset)
            offset = offset + jnp.sum(blk, axis=0, keepdims=True)
        return jnp.concatenate(out, axis=0)

    chosen = jnp.logical_or(above, jnp.logical_and(tie, prefix(tie) < need))
    chosen = jnp.logical_and(chosen, valid)
    slot = jnp.where(chosen, prefix(chosen), -1.0)
    slot_ref[0] = slot
    aff_ref[0] = aff
    slott_ref[0] = slot.T
    afft_ref[0] = aff_t


def _router(logits, cap):
    b, l, _ = logits.shape
    blk = pl.BlockSpec((1, l, LANES), lambda i: (i, 0, 0))
    blk_t = pl.BlockSpec((1, LANES, l), lambda i: (i, 0, 0))
    return pl.pallas_call(
        functools.partial(_router_kernel, cap=cap),
        grid=(b,),
        in_specs=[blk],
        out_specs=[blk, blk, blk_t, blk_t],
        out_shape=[jax.ShapeDtypeStruct((b, l, LANES), F32), jax.ShapeDtypeStruct((b, l, LANES), F32),
                   jax.ShapeDtypeStruct((b, LANES, l), F32), jax.ShapeDtypeStruct((b, LANES, l), F32)],
        compiler_params=_cparams(("parallel",)),
        name="router",
    )(logits)


def _moe_kernel(h_ref, slot_ref, aff_ref, wg_ref, wu_ref, wd_ref, o_ref, *, cap):
    e = pl.program_id(1)
    bs, l, _ = h_ref.shape
    want = lax.broadcasted_iota(jnp.int32, (cap, l), 0).astype(F32)
    onehots, vals, xes = [], [], []
    for s in range(bs):
        hit = want == slot_ref[s, 0]
        onehots.append(jnp.where(hit, 1.0, 0.0).astype(BF16))
        vals.append(jnp.sum(jnp.where(hit, aff_ref[s, 0], 0.0), axis=-1, keepdims=True))
        xes.append(jnp.dot(onehots[s], h_ref[s], preferred_element_type=F32).astype(BF16))
    xe = jnp.concatenate(xes, axis=0)
    hid = (_silu(jnp.dot(xe, wg_ref[0, 0], preferred_element_type=F32))
           * jnp.dot(xe, wu_ref[0, 0], preferred_element_type=F32)).astype(BF16)
    ye = jnp.dot(hid, wd_ref[0, 0], preferred_element_type=F32) * jnp.concatenate(vals, axis=0)
    contribs = [lax.dot_general(onehots[s], ye[s * cap:(s + 1) * cap].astype(BF16), (((0,), (0,)), ((), ())),
                                preferred_element_type=F32) for s in range(bs)]

    @pl.when(e == 0)
    def _():
        for s in range(bs):
            o_ref[s] = contribs[s]

    @pl.when(e != 0)
    def _():
        for s in range(bs):
            o_ref[s] += contribs[s]


def _moe(h2, slot_t, aff_t, wts, layer, cap, bs):
    b, l, d = h2.shape
    _, ne, _, ff = wts['w_gate'].shape
    row = lambda a: a[:, :ne].reshape(b, ne, 1, l)
    rspec = pl.BlockSpec((bs, 1, 1, l), lambda i, j: (i, j, 0, 0))
    tok = pl.BlockSpec((bs, l, d), lambda i, j: (i, 0, 0))
    wspec = lambda r, c: pl.BlockSpec((1, 1, r, c), lambda i, j: (layer, j, 0, 0))
    return pl.pallas_call(
        functools.partial(_moe_kernel, cap=cap),
        grid=(b // bs, ne),
        in_specs=[tok, rspec, rspec, wspec(d, ff), wspec(d, ff), wspec(ff, d)],
        out_specs=tok,
        out_shape=jax.ShapeDtypeStruct((b, l, d), F32),
        compiler_params=_cparams(("parallel", "arbitrary")),
        name="experts",
    )(h2, row(slot_t), row(aff_t), wts['w_gate'], wts['w_up'], wts['w_down'])


def _final_kernel(x_ref, moe_ref, mod_ref, g_ref, o_ref):
    x = x_ref[0] + mod_ref[0][5:6] * moe_ref[0]
    o_ref[0] = x * lax.rsqrt(jnp.mean(x * x, axis=-1, keepdims=True) + NORM_EPS) * g_ref[...]


def _final(x, moe, mod, g, tm):
    b, l, d = x.shape
    tok = pl.BlockSpec((1, tm, d), lambda i, j: (i, j, 0))
    return pl.pallas_call(
        _final_kernel,
        grid=(b, l // tm),
        in_specs=[tok, tok, pl.BlockSpec((1, 8, d), lambda i, j: (i, 0, 0)),
                  pl.BlockSpec((1, d), lambda i, j: (0, 0))],
        out_specs=tok,
        out_shape=jax.ShapeDtypeStruct((b, l, d), F32),
        compiler_params=_cparams(("parallel", "parallel")),
        name="final_norm",
    )(x, moe, mod, g.reshape(1, d))


def _token_block(l):
    return min(l, 512)


_EXPERT_TOKENS_PER_STEP = 2048


def _samples_per_step(b, l):
    bs = max(1, min(b, _EXPERT_TOKENS_PER_STEP // l))
    while b % bs:
        bs -= 1
    return bs


def kernel(x, c, ctx, c_ctx, ada_w, ada_b, norm1_g, norm2_g, w_in, dn_conv_w, dn_a_log, dn_dt_bias, dn_norm_g, hy_conv_w, hy_conv_b, hy_w1, hy_b1, hy_w2, hy_b2, hy_w3, hy_freq, hy_d_bias, sc_conv_w, w_branch_a, w_branch_b, w_branch_c, w_out, router_w, exp_w_gate, exp_w_up, exp_w_down, final_norm_g):
    depth = ada_w.shape[0]
    b, l, d = x.shape
    lc = ctx.shape[1]
    tm_x, tm_c = _token_block(l), _token_block(lc)
    cap_x = CAPACITY_FACTOR * l // N_EXPERTS
    cap_c = CAPACITY_FACTOR * lc // N_EXPERTS

    rows = -(-(b + 1) // 8) * 8
    cvec = jnp.zeros((rows, d), F32).at[:b].set(c).at[b].set(c_ctx)
    mod_all = _modulation(cvec, ada_w, ada_b).reshape(depth, rows, 6, d)
    mod_all = jnp.pad(mod_all, ((0, 0), (0, 0), (0, 2), (0, 0)))
    mods_x = [mod_all[i, :b] for i in range(depth)]
    mods_c = [jnp.broadcast_to(mod_all[i, b], (b, 8, d)) for i in range(depth)]

    tables_x = _dft_tables(l) + _hy_features(l)
    tables_c = _dft_tables(lc) + _hy_features(lc)
    zero_state = jnp.zeros((b, DN_HEADS, DN_HEAD_DIM, DN_HEAD_DIM), F32)

    w_packed = _pack_w_in(w_in)
    wts = {
        'w_a': w_branch_a.astype(BF16), 'w_b': w_branch_b.astype(BF16), 'w_c': w_branch_c.astype(BF16),
        'w_o': w_out.astype(BF16), 'w_r': jnp.pad(router_w, ((0, 0), (0, 0), (0, LANES - N_EXPERTS))),
        'w_gate': exp_w_gate.astype(BF16), 'w_up': exp_w_up.astype(BF16), 'w_down': exp_w_down.astype(BF16),
    }

    moe_x = moe_c = None
    for i in range(depth):
        lp = {
            'dn_a_log': dn_a_log[i], 'dn_dt_bias': dn_dt_bias[i],
            'dn_norm_g': dn_norm_g[i], 'hy_conv_w': hy_conv_w[i], 'hy_conv_b': hy_conv_b[i],
            'hy_w1': hy_w1[i], 'hy_b1': hy_b1[i], 'hy_w2': hy_w2[i], 'hy_b2': hy_b2[i], 'hy_w3': hy_w3[i],
            'hy_freq': hy_freq[i], 'hy_d_bias': hy_d_bias[i], 'sc_conv_w': sc_conv_w[i],
            'norm2_g': norm2_g[i],
        }
        last = i == depth - 1

        def mixer_inputs(tokens, moe, mods, tm, row_len):
            if moe is None:
                outs = _inproj(tokens, mods[i], norm1_g[i], w_packed, dn_conv_w, i, tm, row_len)
                return tokens, outs
            outs = _inproj(tokens, mods[i], norm1_g[i], w_packed, dn_conv_w, i, tm, row_len,
                           moe=moe, mod_prev=mods[i - 1])
            return outs[-1], outs[:-1]

        def rest_of_layer(tokens, parts, ya, mods, tm, row_len, tables, cap, samples_per_step):
            _, _, hy, sc, gates, _ = parts
            filt = _hy_filter(tokens.shape[1], lp, tables)
            yb = _hyena(hy, lp, filt, tables, row_len)
            tokens, h2, logits = _merge(tokens, ya, yb, sc, gates, mods[i], lp, wts, i, tm, row_len)
            _, _, slot_t, aff_t = _router(logits, cap)
            return tokens, _moe(h2, slot_t, aff_t, wts, i, cap, samples_per_step)

        ctx, parts_c = mixer_inputs(ctx, moe_c, mods_c, tm_c, lc)
        gb_c = _dn_gates(parts_c[5], lp['dn_a_log'], lp['dn_dt_bias'])
        ya_c, s_f, s_b = _deltanet(parts_c[0], parts_c[1], gb_c, lp['dn_norm_g'], zero_state, zero_state)
        if not last:
            ctx, moe_c = rest_of_layer(ctx, parts_c, ya_c, mods_c, tm_c, lc, tables_c, cap_c, _samples_per_step(b, lc))

        x, parts_x = mixer_inputs(x, moe_x, mods_x, tm_x, GRID_W)
        gb_x = _dn_gates(parts_x[5], lp['dn_a_log'], lp['dn_dt_bias'])
        ya_x, _, _ = _deltanet(parts_x[0], parts_x[1], gb_x, lp['dn_norm_g'], s_f, s_b)
        x, moe_x = rest_of_layer(x, parts_x, ya_x, mods_x, tm_x, GRID_W, tables_x, cap_x, _samples_per_step(b, l))

    return _final(x, moe_x, mods_x[depth - 1], final_norm_g, tm_x)
```

```python
import functools
import math

import jax
import jax.numpy as jnp
import numpy as np
from jax import lax
from jax.experimental import pallas as pl
from jax.experimental.pallas import tpu as pltpu

F32 = jnp.float32
BF16 = jnp.bfloat16
HIGHEST = lax.Precision.HIGHEST

GRID_W = 64
DN_HEADS = 4
DN_HEAD_DIM = 128
DN_WIDTH = DN_HEADS * DN_HEAD_DIM
DN_CHUNK = 64
HY_WIDTH = 256
HY_BANDS = 16
HY_TARGET = 1e-2
HY_FAST = 0.3
HY_SLOW = 1.5
SC_WIDTH = 256
N_BRANCH = 3
N_EXPERTS = 16
CAPACITY_FACTOR = 2
NORM_EPS = 1e-6

LANES = 128
VMEM_LIMIT = 56 * 1024 * 1024


def _cparams(sem):
    return pltpu.CompilerParams(dimension_semantics=sem, vmem_limit_bytes=VMEM_LIMIT)


def _mm(a, b):
    return jnp.dot(a.astype(BF16), b.astype(BF16), preferred_element_type=F32)


def _mm_nt(a, b):
    return lax.dot_general(a.astype(BF16), b.astype(BF16), (((1,), (1,)), ((), ())),
                           preferred_element_type=F32)


def _mm_tn(a, b):
    return lax.dot_general(a.astype(BF16), b.astype(BF16), (((0,), (0,)), ((), ())),
                           preferred_element_type=F32)


def _mm_f32(a, b):
    return jnp.dot(a, b, precision=HIGHEST, preferred_element_type=F32)


def _split_bf16(b):
    hi = b.astype(BF16)
    return hi, (b - hi.astype(F32)).astype(BF16)


def _sigmoid(x):
    return 0.5 * jnp.tanh(0.5 * x) + 0.5


def _silu(x):
    return x * _sigmoid(x)


def _softplus(x):
    return jnp.maximum(x, 0.0) + jnp.log(1.0 + jnp.exp(-jnp.abs(x)))


def _conv3(x, w, row_len):
    t = x.shape[0]
    pos = lax.broadcasted_iota(jnp.int32, x.shape, 0) & (row_len - 1)
    prev = jnp.where(pos == 0, 0.0, pltpu.roll(x, 1, axis=0))
    nxt = jnp.where(pos == row_len - 1, 0.0, pltpu.roll(x, t - 1, axis=0))
    return prev * w[0:1] + x * w[1:2] + nxt * w[2:3]


def _mod_kernel(c_ref, w_ref, b_ref, o_ref):
    o_ref[0] = _mm_f32(_silu(c_ref[...]), w_ref[0]) + b_ref[0]


def _modulation(cvec, ada_w, ada_b):
    depth, d, n6 = ada_w.shape
    rows = cvec.shape[0]
    tn = 1536
    return pl.pallas_call(
        _mod_kernel,
        grid=(depth, n6 // tn),
        in_specs=[pl.BlockSpec((rows, d), lambda i, j: (0, 0)),
                  pl.BlockSpec((1, d, tn), lambda i, j: (i, 0, j)),
                  pl.BlockSpec((1, 1, tn), lambda i, j: (i, 0, j))],
        out_specs=pl.BlockSpec((1, rows, tn), lambda i, j: (i, 0, j)),
        out_shape=jax.ShapeDtypeStruct((depth, rows, n6), F32),
        compiler_params=_cparams(("parallel", "parallel")),
        name="modulation",
    )(cvec, ada_w, ada_b.reshape(depth, 1, n6))


_QKV = (0, 3 * DN_WIDTH)
_Z = (_QKV[0] + _QKV[1], DN_WIDTH)
_BA = (_Z[0] + _Z[1], LANES)
_HY = (_BA[0] + _BA[1], 3 * HY_WIDTH)
_SC = (_HY[0] + _HY[1], 3 * SC_WIDTH)
_GATES = (_SC[0] + _SC[1], N_BRANCH * 1024)
N_PACKED = _GATES[0] + _GATES[1]


def _pack_w_in(w_in):
    used = _BA[0] + 4 * DN_HEADS
    w = w_in.astype(BF16)
    pad = jnp.zeros(w.shape[:-1] + (_BA[1] - 4 * DN_HEADS,), BF16)
    return jnp.concatenate([w[..., :used], pad, w[..., used:]], axis=-1)


def _rms_mod(x, g, scale, shift):
    y = x * lax.rsqrt(jnp.mean(x * x, axis=-1, keepdims=True) + NORM_EPS)
    return (y * g) * (1.0 + scale) + shift


def _inproj_kernel(*refs, has_moe, row_len):
    if has_moe:
        x_ref, moe_ref, modp_ref, mod_ref, g_ref, w_ref, cw_ref = refs[:7]
        outs = refs[7:]
        xo_ref = outs[-1]
        x = x_ref[0] + modp_ref[0][5:6] * moe_ref[0]
        xo_ref[0] = x
    else:
        x_ref, mod_ref, g_ref, w_ref, cw_ref = refs[:5]
        outs = refs[5:]
        x = x_ref[0]
    qkv_ref, z_ref, hy_ref, sc_ref, gates_ref, ba_ref = outs[:6]
    m = mod_ref[0]
    hb = _rms_mod(x, g_ref[...], m[1:2], m[0:1]).astype(BF16)

    def proj(sec, lo=0, width=None):
        a = sec[0] + lo
        wd = sec[1] if width is None else width
        return jnp.dot(hb, w_ref[0, :, a:a + wd], preferred_element_type=F32)

    act = _silu(_conv3(proj(_QKV), cw_ref[0], row_len))
    hd = DN_HEAD_DIM
    for j in range(3 * DN_HEADS):
        t = act[:, j * hd:(j + 1) * hd]
        if j < 2 * DN_HEADS:
            t = t * lax.rsqrt(jnp.sum(t * t, axis=-1, keepdims=True) + 1e-6)
        if j < DN_HEADS:
            t = t * (hd ** -0.5)
        qkv_ref[0, :, j * hd:(j + 1) * hd] = t.astype(BF16)
    z_ref[0] = proj(_Z).astype(BF16)
    hy_ref[0] = proj(_HY).astype(BF16)
    sc_ref[0] = proj(_SC).astype(BF16)
    for j in range(N_BRANCH):
        gates_ref[0, :, j * 1024:(j + 1) * 1024] = proj(_GATES, j * 1024, 1024).astype(BF16)
    ba_ref[0] = proj(_BA)


def _inproj(x, mod, norm_g, w_packed, dn_conv_w, layer, tm, row_len, moe=None, mod_prev=None):
    b, l, d = x.shape
    assert tm % row_len == 0
    has_moe = moe is not None
    tok = pl.BlockSpec((1, tm, d), lambda i, j: (i, j, 0))
    modspec = pl.BlockSpec((1, 8, d), lambda i, j: (i, 0, 0))
    in_specs = [tok]
    args = [x]
    if has_moe:
        in_specs += [tok, modspec]
        args += [moe, mod_prev]
    in_specs += [modspec,
                 pl.BlockSpec((1, d), lambda i, j: (0, 0)),
                 pl.BlockSpec((1, d, N_PACKED), lambda i, j: (layer, 0, 0), pipeline_mode=pl.Buffered(1)),
                 pl.BlockSpec((1,) + dn_conv_w.shape[1:], lambda i, j: (layer, 0, 0))]
    args += [mod, norm_g.reshape(1, d), w_packed, dn_conv_w]
    widths = [(_QKV[1], BF16), (_Z[1], BF16), (_HY[1], BF16), (_SC[1], BF16), (_GATES[1], BF16), (_BA[1], F32)]
    out_specs = [pl.BlockSpec((1, tm, w), lambda i, j: (i, j, 0)) for w, _ in widths]
    out_shape = [jax.ShapeDtypeStruct((b, l, w), dt) for w, dt in widths]
    if has_moe:
        out_specs.append(tok)
        out_shape.append(jax.ShapeDtypeStruct((b, l, d), F32))
    return pl.pallas_call(
        functools.partial(_inproj_kernel, has_moe=has_moe, row_len=row_len),
        grid=(b, l // tm),
        in_specs=in_specs,
        out_specs=out_specs,
        out_shape=out_shape,
        compiler_params=_cparams(("parallel", "parallel")),
        name="inproj",
    )(*args)


_BETA_LANE = 0
_DECAY_LANE = 2 * DN_HEADS
_BWD_DECAY_LANE = _DECAY_LANE + DN_HEADS


def _dn_gates_kernel(ba_ref, alog_ref, dtb_ref, o_ref, *, nc):
    x = ba_ref[0]
    beta = _sigmoid(x)
    g = -jnp.exp(alog_ref[...]) * _softplus(x + dtb_ref[...])
    ii = lax.broadcasted_iota(jnp.int32, (DN_CHUNK, DN_CHUNK), 0)
    jj = lax.broadcasted_iota(jnp.int32, (DN_CHUNK, DN_CHUNK), 1)
    lower = jnp.where(ii >= jj, 1.0, 0.0).astype(F32)
    upper = jnp.where(ii <= jj, 1.0, 0.0).astype(F32)
    lane = lax.broadcasted_iota(jnp.int32, (DN_CHUNK, LANES), 1)
    for c in range(nc):
        rows = slice(c * DN_CHUNK, (c + 1) * DN_CHUNK)
        gch = g[rows]
        acc = jnp.where(lane >= _BWD_DECAY_LANE, _mm_f32(upper, gch), _mm_f32(lower, gch))
        o_ref[0, rows, :] = jnp.where(lane < _DECAY_LANE, beta[rows], acc)


def _dn_gates(ba, a_log, dt_bias):
    b, l, _ = ba.shape
    nc = l // DN_CHUNK
    vec = lambda p: jnp.zeros((1, LANES), F32).at[0, _DECAY_LANE:_DECAY_LANE + 2 * DN_HEADS].set(p.reshape(-1))
    return pl.pallas_call(
        functools.partial(_dn_gates_kernel, nc=nc),
        grid=(b,),
        in_specs=[pl.BlockSpec((1, l, LANES), lambda i: (i, 0, 0)),
                  pl.BlockSpec((1, LANES), lambda i: (0, 0)),
                  pl.BlockSpec((1, LANES), lambda i: (0, 0))],
        out_specs=pl.BlockSpec((1, l, LANES), lambda i: (i, 0, 0)),
        out_shape=jax.ShapeDtypeStruct((b, l, LANES), F32),
        compiler_params=_cparams(("parallel",)),
        name="dn_gates",
    )(ba, vec(a_log), vec(dt_bias))


_GROUP_CHUNKS = 4


def _dn_kernel(qkv_ref, z_ref, gb_ref, gbt_ref, ng_ref, s0f_ref, s0b_ref, ya_ref, sf_ref, sb_ref,
               of_s, ob_s, st_s, *, ngroups, gc):
    c, hd, nh = DN_CHUNK, DN_HEAD_DIM, DN_HEADS
    gt = gc * c
    shift = c.bit_length() - 1

    for h in range(nh):
        st_s[h] = s0f_ref[0, h]
        st_s[nh + h] = s0b_ref[0, h]

    rr = lax.broadcasted_iota(jnp.int32, (c, gt), 0)
    ln = lax.broadcasted_iota(jnp.int32, (c, gt), 1)
    lblk = ln >> shift
    cc = ln & (c - 1)
    eye_p = jnp.where(rr == cc, 1.0, 0.0).astype(F32)
    inside = [(rr >> lv) == (cc >> lv) for lv in range(1, shift + 1)]
    level_masks = [jnp.where(inside[0], 1.0, 0.0).astype(F32)] + [
        jnp.where(jnp.logical_and(inside[j], jnp.logical_not(inside[j - 1])), 1.0, 0.0).astype(F32)
        for j in range(1, shift)]
    same_blk = ((lax.broadcasted_iota(jnp.int32, (gt, gt), 0) >> shift)
                == (lax.broadcasted_iota(jnp.int32, (gt, gt), 1) >> shift))
    blk_mask = jnp.where(same_blk, 1.0, 0.0).astype(BF16)
    rblk = lax.broadcasted_iota(jnp.int32, (gt, 1), 0) >> shift

    def pack_cols(col):
        out = jnp.broadcast_to(col[0:c], (c, gt))
        for i in range(1, gc):
            out = jnp.where(lblk == i, jnp.broadcast_to(col[i * c:(i + 1) * c], (c, gt)), out)
        return out

    def pack_diag(full):
        out = full[0:c]
        for i in range(1, gc):
            out = jnp.where(lblk == i, full[i * c:(i + 1) * c], out)
        return out

    def block_diag(packed_bf16):
        return jnp.concatenate([packed_bf16] * gc, axis=0) * blk_mask

    pair_lane = lax.broadcasted_iota(jnp.int32, (1, 2 * hd), 1)
    first_head = jnp.where(pair_lane < hd, 1.0, 0.0).astype(BF16)
    second_head = jnp.where(pair_lane >= hd, 1.0, 0.0).astype(BF16)

    def pair_diag(pair_bf16):
        return jnp.concatenate([pair_bf16 * first_head, pair_bf16 * second_head], axis=0)

    def dotf(a, b):
        return jnp.dot(a, b, preferred_element_type=F32)

    def prepare(h, d, g):
        r0 = pl.multiple_of(g * gt, gt)
        part = lambda p: qkv_ref[0, pl.ds(r0, gt), (p * nh + h) * hd:(p * nh + h + 1) * hd].astype(F32)
        qg, kg, vg = part(0), part(1), part(2)
        gbg = gb_ref[0, pl.ds(r0, gt), :]
        col = d * nh + h
        beta = gbg[:, _BETA_LANE + col:_BETA_LANE + col + 1]
        gcc = gbg[:, _DECAY_LANE + col:_DECAY_LANE + col + 1]
        gcr = gbt_ref[0, g][_DECAY_LANE + col:_DECAY_LANE + col + 1, :]
        if d == 0:
            incl, strict, last = rr >= cc, rr > cc, c - 1
        else:
            incl, strict, last = rr <= cc, rr < cc, 0
        kq = _mm_nt(jnp.concatenate([kg, qg], axis=0), kg)
        decay = jnp.where(incl, jnp.exp(jnp.where(incl, pack_cols(gcc) - gcr, 0.0)), 0.0)
        m = jnp.where(strict, pack_diag(kq[:gt]) * pack_cols(beta) * decay, 0.0)
        attn = pack_diag(kq[gt:]) * decay
        eg = jnp.exp(gcc)
        glast = jnp.broadcast_to(gcc[last:last + 1], (gt, 1))
        for i in range(1, gc):
            glast = jnp.where(rblk == i, gcc[i * c + last:i * c + last + 1], glast)
        return dict(m=m, attn=attn.astype(BF16), qd=(qg * eg).astype(BF16),
                    kd=(kg * jnp.exp(glast - gcc)).astype(BF16),
                    rhs=jnp.concatenate([vg * beta, kg * (beta * eg)], axis=1).astype(BF16),
                    chunk_decay=[jnp.exp(gcc[i * c + last:i * c + last + 1]) for i in range(gc)])

    def body(j, carry):
        groups = (j, ngroups - 1 - j)
        slots = [(h, d) for h in range(nh) for d in range(2)]
        state = [st_s[d * nh + h] for h, d in slots]
        pre = [prepare(h, d, groups[d]) for h, d in slots]
        ts = [eye_p - p['m'] * level_masks[0] for p in pre]
        for level_mask in level_masks[1:]:
            xs = [dotf((p['m'] * level_mask).astype(BF16), block_diag(t.astype(BF16))) for p, t in zip(pre, ts)]
            ts = [t - dotf(t.astype(BF16), block_diag(x.astype(BF16))) for t, x in zip(ts, xs)]
        uws = [dotf(block_diag(t.astype(BF16)), p['rhs']) for p, t in zip(pre, ts)]
        lanes = lambda parts: jnp.concatenate(parts, axis=1)
        index = {slot: n for n, slot in enumerate(slots)}
        pairs = [(hp, d) for hp in range(nh // 2) for d in range(2)]
        packed = []
        for hp, d in pairs:
            na, nb = index[(2 * hp, d)], index[(2 * hp + 1, d)]
            packed.append(dict(
                s=lanes([state[na], state[nb]]),
                u=lanes([uws[na][:, :hd], uws[nb][:, :hd]]),
                w=lanes([uws[na][:, hd:], uws[nb][:, hd:]]).astype(BF16),
                qd=lanes([pre[na]['qd'], pre[nb]['qd']]), a=pre[na], b=pre[nb]))
        outs = [[None] * gc for _ in pairs]
        for step in range(gc):
            for n, (hp, d) in enumerate(pairs):
                i = step if d == 0 else gc - 1 - step
                rows = slice(i * c, (i + 1) * c)
                q = packed[n]
                wq = jnp.concatenate([q['w'][rows], q['qd'][rows]], axis=0)
                a = dotf(wq, pair_diag(q['s'].astype(BF16)))
                v_new = (q['u'][rows] - a[:c]).astype(BF16)
                lhs = jnp.concatenate([lanes([q['a']['attn'][:, rows], q['b']['attn'][:, rows]]),
                                       lanes([q['a']['kd'][rows].T, q['b']['kd'][rows].T])], axis=0)
                av = dotf(lhs, pair_diag(v_new))
                outs[n][i] = a[c:] + av[:c]
                decay = lanes([jnp.broadcast_to(q['a']['chunk_decay'][i], (1, hd)),
                               jnp.broadcast_to(q['b']['chunk_decay'][i], (1, hd))])
                q['s'] = q['s'] * decay + av[c:]
        for n, (hp, d) in enumerate(pairs):
            st_s[d * nh + 2 * hp] = packed[n]['s'][:, :hd]
            st_s[d * nh + 2 * hp + 1] = packed[n]['s'][:, hd:]
            r0 = pl.multiple_of(groups[d] * gt, gt)
            (of_s, ob_s)[d][pl.ds(r0, gt), 2 * hp * hd:(2 * hp + 2) * hd] = jnp.concatenate(outs[n], axis=0)
        return carry

    lax.fori_loop(0, ngroups, body, 0)
    for h in range(nh):
        sf_ref[0, h] = st_s[h]
        sb_ref[0, h] = st_s[nh + h]
        hs = slice(h * hd, (h + 1) * hd)
        o = of_s[:, hs] + ob_s[:, hs]
        y = o * lax.rsqrt(jnp.mean(o * o, axis=-1, keepdims=True) + NORM_EPS) * ng_ref[...]
        ya_ref[0, :, hs] = (y * _silu(z_ref[0, :, hs].astype(F32))).astype(BF16)


def _deltanet(qkv, z, gb, norm_g, s0f, s0b):
    b, l, _ = qkv.shape
    c, hd, nh = DN_CHUNK, DN_HEAD_DIM, DN_HEADS
    gc = min(_GROUP_CHUNKS, l // c)
    gt = gc * c
    ngroups = l // gt
    nlanes = 4 * nh
    gbt = jnp.transpose(gb[:, :, :nlanes].reshape(b, ngroups, gt, nlanes), (0, 1, 3, 2))
    st_blk = pl.BlockSpec((1, nh, hd, hd), lambda i: (i, 0, 0, 0))
    seq_blk = lambda w: pl.BlockSpec((1, l, w), lambda i: (i, 0, 0))
    return pl.pallas_call(
        functools.partial(_dn_kernel, ngroups=ngroups, gc=gc),
        grid=(b,),
        in_specs=[seq_blk(3 * DN_WIDTH), seq_blk(DN_WIDTH), seq_blk(LANES),
                  pl.BlockSpec((1, ngroups, nlanes, gt), lambda i: (i, 0, 0, 0)),
                  pl.BlockSpec((1, hd), lambda i: (0, 0)),
                  st_blk, st_blk],
        out_specs=[seq_blk(DN_WIDTH), st_blk, st_blk],
        out_shape=[jax.ShapeDtypeStruct((b, l, DN_WIDTH), BF16),
                   jax.ShapeDtypeStruct((b, nh, hd, hd), F32),
                   jax.ShapeDtypeStruct((b, nh, hd, hd), F32)],
        scratch_shapes=[pltpu.VMEM((l, DN_WIDTH), F32),
                        pltpu.VMEM((l, DN_WIDTH), F32),
                        pltpu.VMEM((2 * nh, hd, hd), F32)],
        compiler_params=_cparams(("parallel",)),
        name="deltanet",
    )(qkv, z, gb, gbt, norm_g.reshape(1, hd), s0f, s0b)


def _dft_tables(l):
    n = 2 * l
    nb = GRID_W
    f = jnp.arange(l, dtype=jnp.int32)[:, None]
    angle = lambda k: (k % n).astype(F32) * (2.0 * math.pi / n)
    ang_a = angle(f * (nb * jnp.arange(l // nb, dtype=jnp.int32))[None, :])[:, :, None]
    ang_b = angle(f * jnp.arange(nb, dtype=jnp.int32)[None, :])[:, None, :]
    ca, sa, cb, sb = jnp.cos(ang_a), jnp.sin(ang_a), jnp.cos(ang_b), jnp.sin(ang_b)
    return ((ca * cb - sa * sb).reshape(l, l).astype(BF16), (sa * cb + ca * sb).reshape(l, l).astype(BF16))


def _hy_features(l):
    pos = jnp.arange(l, dtype=F32)
    t = (pos / max(l - 1, 1))[:, None]
    bands = jnp.linspace(1e-4, HY_BANDS - 1, HY_BANDS, dtype=F32)
    ang = (2.0 * math.pi / l) * pos[:, None] * bands[None, :]
    feats = jnp.concatenate([t, jnp.cos(ang), -jnp.sin(ang)], axis=-1)
    feats = jnp.pad(feats, ((0, 0), (0, LANES - feats.shape[1])))
    deltas = jnp.abs(jnp.linspace(math.log(HY_TARGET) / HY_SLOW, math.log(HY_TARGET) / HY_FAST,
                                  HY_WIDTH, dtype=F32))
    return feats, t, deltas[None, :]


def _hy_filter_kernel(feats_ref, t_ref, deltas_ref, w1_ref, b1_ref, w2_ref, b2_ref, w3_ref, freq_ref,
                      cos_ref, sin_ref, ka_ref, kb_ref, kn_ref):
    l = feats_ref.shape[0]
    n = 2 * l
    freq = freq_ref[...]
    hh = jnp.sin(freq * (_mm_f32(feats_ref[...], w1_ref[...]) + b1_ref[...]))
    hh = jnp.sin(freq * (_mm_f32(hh, w2_ref[...]) + b2_ref[...]))
    hh = _mm_f32(hh, w3_ref[...])
    window = jnp.exp(-t_ref[...] * deltas_ref[...])
    h_fwd = hh[:, :HY_WIDTH] * window
    h_bwd = hh[:, HY_WIDTH:] * window
    row = lax.broadcasted_iota(jnp.int32, (l, HY_WIDTH), 0)
    h_bwd = jnp.where(row == 0, 0.0, h_bwd)
    hs = h_fwd + h_bwd
    hd = h_fwd - h_bwd
    hs_hi, hs_lo = _split_bf16(hs)
    hd_hi, hd_lo = _split_bf16(hd)
    blk = min(_DFT_ROW_BLOCK, l)
    brow = lax.broadcasted_iota(jnp.int32, (blk, HY_WIDTH), 0)
    for i in range(l // blk):
        rows = slice(i * blk, (i + 1) * blk)
        cos_b = cos_ref[rows, :]
        sin_b = sin_ref[rows, :]
        ka = jnp.dot(cos_b, hs_hi, preferred_element_type=F32) + jnp.dot(cos_b, hs_lo, preferred_element_type=F32)
        kb = jnp.dot(sin_b, hd_hi, preferred_element_type=F32) + jnp.dot(sin_b, hd_lo, preferred_element_type=F32)
        ka_ref[rows, :] = ka * (jnp.where(brow == 0, 1.0 / n, 2.0 / n) if i == 0 else 2.0 / n)
        kb_ref[rows, :] = kb * (2.0 / n)
    nyq = jnp.sum(jnp.where((row & 1) == 0, hs, -hs), axis=0, keepdims=True) * (1.0 / n)
    kn_ref[...] = jnp.broadcast_to(nyq, kn_ref.shape)


def _hy_filter(l, lp, tables):
    cos_t, sin_t, feats, t, deltas = tables
    w1 = jnp.pad(lp['hy_w1'], ((0, LANES - lp['hy_w1'].shape[0]), (0, 0)))
    row = lambda v: v.reshape(1, -1)
    args = (feats, t, deltas, w1, row(lp['hy_b1']), lp['hy_w2'], row(lp['hy_b2']), lp['hy_w3'],
            row(lp['hy_freq']), cos_t, sin_t)
    return pl.pallas_call(
        _hy_filter_kernel,
        out_shape=[jax.ShapeDtypeStruct((l, HY_WIDTH), F32),
                   jax.ShapeDtypeStruct((l, HY_WIDTH), F32),
                   jax.ShapeDtypeStruct((8, HY_WIDTH), F32)],
        compiler_params=pltpu.CompilerParams(vmem_limit_bytes=VMEM_LIMIT),
        name="hyena_filter",
    )(*args)


_DFT_ROW_BLOCK = 256


def _hyena_kernel(hy_ref, cw_ref, cb_ref, db_ref, cos_ref, sin_ref, ka_ref, kb_ref, kn_ref, o_ref, *, row_len):
    l = hy_ref.shape[1]
    u = _conv3(hy_ref[0].astype(F32), cw_ref[...], row_len) + cb_ref[...]
    x0 = u[:, :HY_WIDTH]
    z = u[:, HY_WIDTH:2 * HY_WIDTH] * u[:, 2 * HY_WIDTH:]
    zb = z.astype(BF16)
    blk = min(_DFT_ROW_BLOCK, l)
    blocks = [slice(i * blk, (i + 1) * blk) for i in range(l // blk)]
    ya, yb = [], []
    for rows in blocks:
        xa = jnp.dot(cos_ref[rows, :], zb, preferred_element_type=F32)
        xb = jnp.dot(sin_ref[rows, :], zb, preferred_element_type=F32)
        ka = ka_ref[rows, :]
        kb = kb_ref[rows, :]
        ya.append((xa * ka - xb * kb).astype(BF16))
        yb.append((xa * kb + xb * ka).astype(BF16))
    ya = jnp.concatenate(ya, axis=0)
    yb = jnp.concatenate(yb, axis=0)
    even = (lax.broadcasted_iota(jnp.int32, (blk, HY_WIDTH), 0) & 1) == 0
    sign = jnp.where((lax.broadcasted_iota(jnp.int32, (l, HY_WIDTH), 0) & 1) == 0, 1.0, -1.0)
    y_nyq = jnp.sum(z * sign, axis=0, keepdims=True) * kn_ref[0:1]
    for rows in blocks:
        y = (jnp.dot(cos_ref[rows, :], ya, preferred_element_type=F32)
             + jnp.dot(sin_ref[rows, :], yb, preferred_element_type=F32))
        y = y + jnp.where(even, y_nyq, -y_nyq)
        o_ref[0, rows, :] = (x0[rows] * (y + z[rows] * db_ref[...])).astype(BF16)


def _hyena(hy, lp, filt, tables, row_len):
    b, l, w3 = hy.shape
    ka, kb, kn = filt
    cos_t, sin_t = tables[0], tables[1]
    const = lambda shape: pl.BlockSpec(shape, lambda i: (0,) * len(shape), pipeline_mode=pl.Buffered(1))
    return pl.pallas_call(
        functools.partial(_hyena_kernel, row_len=row_len),
        grid=(b,),
        in_specs=[pl.BlockSpec((1, l, w3), lambda i: (i, 0, 0)),
                  const((3, w3)), const((1, w3)), const((1, HY_WIDTH)),
                  const((l, l)), const((l, l)),
                  const((l, HY_WIDTH)), const((l, HY_WIDTH)), const((8, HY_WIDTH))],
        out_specs=pl.BlockSpec((1, l, HY_WIDTH), lambda i: (i, 0, 0)),
        out_shape=jax.ShapeDtypeStruct((b, l, HY_WIDTH), BF16),
        compiler_params=_cparams(("parallel",)),
        name="hyena",
    )(hy, lp['hy_conv_w'], lp['hy_conv_b'].reshape(1, w3), lp['hy_d_bias'].reshape(1, HY_WIDTH),
      cos_t, sin_t, ka, kb, kn)


def _merge_kernel(x_ref, ya_ref, yb_ref, sc_ref, gates_ref, mod_ref, n2_ref, scw_ref, wa_ref, wb_ref,
                  wc_ref, wo_ref, wr_ref, xo_ref, h2_ref, lg_ref, *, row_len, parts):
    tm, d = x_ref.shape[1:]
    m = mod_ref[0]
    w_hi, w_lo = _split_bf16(wr_ref[0])
    for part in range(parts):
        rows = slice(part * (tm // parts), (part + 1) * (tm // parts))
        sc = sc_ref[0, rows, :].astype(F32)
        yc = sc[:, :SC_WIDTH] * _conv3(sc[:, SC_WIDTH:2 * SC_WIDTH] * sc[:, 2 * SC_WIDTH:], scw_ref[...], row_len)
        gate = lambda j: _sigmoid(gates_ref[0, rows, j * d:(j + 1) * d].astype(F32))
        merged = (gate(0) * jnp.dot(ya_ref[0, rows, :], wa_ref[0], preferred_element_type=F32)
                  + gate(1) * jnp.dot(yb_ref[0, rows, :], wb_ref[0], preferred_element_type=F32)
                  + gate(2) * _mm(yc, wc_ref[0]))
        x = x_ref[0, rows, :] + m[2:3] * _mm(merged, wo_ref[0])
        xo_ref[0, rows, :] = x
        h2 = _rms_mod(x, n2_ref[...], m[4:5], m[3:4])
        h2_ref[0, rows, :] = h2.astype(BF16)
        h_hi, h_lo = _split_bf16(h2)
        lg_ref[0, rows, :] = (jnp.dot(h_hi, w_hi, preferred_element_type=F32)
                              + jnp.dot(h_lo, w_hi, preferred_element_type=F32)
                              + jnp.dot(h_hi, w_lo, preferred_element_type=F32))


_MERGE_PART_ROWS = 256


def _merge(x, ya, yb, sc, gates, mod, lp, wts, layer, tm, row_len):
    b, l, d = x.shape
    parts = max(1, tm // max(row_len, _MERGE_PART_ROWS))
    tok = lambda w: pl.BlockSpec((1, tm, w), lambda i, j: (i, j, 0))
    const = lambda shape: pl.BlockSpec(shape, lambda i, j: (0,) * len(shape), pipeline_mode=pl.Buffered(1))
    per_layer = lambda r, c: pl.BlockSpec((1, r, c), lambda i, j: (layer, 0, 0), pipeline_mode=pl.Buffered(1))
    return pl.pallas_call(
        functools.partial(_merge_kernel, row_len=row_len, parts=parts),
        grid=(b, l // tm),
        in_specs=[tok(d), tok(DN_WIDTH), tok(HY_WIDTH), tok(3 * SC_WIDTH), tok(N_BRANCH * d),
                  pl.BlockSpec((1, 8, d), lambda i, j: (i, 0, 0)),
                  const((1, d)), const((3, SC_WIDTH)),
                  per_layer(DN_WIDTH, d), per_layer(HY_WIDTH, d), per_layer(SC_WIDTH, d), per_layer(d, d),
                  per_layer(d, LANES)],
        out_specs=[tok(d), tok(d), tok(LANES)],
        out_shape=[jax.ShapeDtypeStruct((b, l, d), F32),
                   jax.ShapeDtypeStruct((b, l, d), BF16),
                   jax.ShapeDtypeStruct((b, l, LANES), F32)],
        compiler_params=_cparams(("parallel", "parallel")),
        name="merge",
    )(x, ya, yb, sc, gates, mod, lp['norm2_g'].reshape(1, d), lp['sc_conv_w'],
      wts['w_a'], wts['w_b'], wts['w_c'], wts['w_o'], wts['w_r'])


_PREFIX_BLOCK = 256


def _router_kernel(lg_ref, slot_ref, aff_ref, slott_ref, afft_ref, *, cap):
    l = lg_ref.shape[1]
    lane = lax.broadcasted_iota(jnp.int32, (l, LANES), 1)
    valid = lane < N_EXPERTS
    lg = jnp.where(valid, lg_ref[0], -jnp.inf)
    e = jnp.where(valid, jnp.exp(lg - jnp.max(lg, axis=-1, keepdims=True)), 0.0)
    aff = e / jnp.sum(e, axis=-1, keepdims=True)

    def count(mask):
        return jnp.sum(jnp.where(mask, 1.0, 0.0), axis=0, keepdims=True)

    aff_t = aff.T
    dense = aff_t[:N_EXPERTS]

    def bisect(i, lo):
        cand = lo | jnp.left_shift(jnp.int32(1), 30 - i)
        n_ge = jnp.sum(jnp.where(dense >= pltpu.bitcast(cand, F32)[:, :1], 1.0, 0.0), axis=-1, keepdims=True)
        return jnp.where(n_ge >= cap, cand, lo)

    thr_t = pltpu.bitcast(lax.fori_loop(0, 31, bisect, jnp.zeros((N_EXPERTS, LANES), jnp.int32)), F32)
    on_diag = (lax.broadcasted_iota(jnp.int32, (N_EXPERTS, LANES), 0)
               == lax.broadcasted_iota(jnp.int32, (N_EXPERTS, LANES), 1))
    thr = jnp.sum(jnp.where(on_diag, thr_t, 0.0), axis=0, keepdims=True)
    above = aff > thr
    tie = aff == thr
    need = cap - count(above)

    pb = _PREFIX_BLOCK if l % _PREFIX_BLOCK == 0 else l
    ii = lax.broadcasted_iota(jnp.int32, (pb, pb), 0)
    jj = lax.broadcasted_iota(jnp.int32, (pb, pb), 1)
    before = jnp.where(ii > jj, 1.0, 0.0).astype(BF16)

    def prefix(mask):
        vals = jnp.where(mask, 1.0, 0.0)
        out = []
        offset = jnp.zeros((1, LANES), F32)
        for r in range(l // pb):
            blk = vals[r * pb:(r + 1) * pb]
            out.append(jnp.dot(before, blk.astype(BF16), preferred_element_type=F32) + offset)
            offset = offset + jnp.sum(blk, axis=0, keepdims=True)
        return jnp.concatenate(out, axis=0)

    chosen = jnp.logical_or(above, jnp.logical_and(tie, prefix(tie) < need))
    chosen = jnp.logical_and(chosen, valid)
    slot = jnp.where(chosen, prefix(chosen), -1.0)
    slot_ref[0] = slot
    aff_ref[0] = aff
    slott_ref[0] = slot.T
    afft_ref[0] = aff_t


def _router(logits, cap):
    b, l, _ = logits.shape
    blk = pl.BlockSpec((1, l, LANES), lambda i: (i, 0, 0))
    blk_t = pl.BlockSpec((1, LANES, l), lambda i: (i, 0, 0))
    return pl.pallas_call(
        functools.partial(_router_kernel, cap=cap),
        grid=(b,),
        in_specs=[blk],
        out_specs=[blk, blk, blk_t, blk_t],
        out_shape=[jax.ShapeDtypeStruct((b, l, LANES), F32), jax.ShapeDtypeStruct((b, l, LANES), F32),
                   jax.ShapeDtypeStruct((b, LANES, l), F32), jax.ShapeDtypeStruct((b, LANES, l), F32)],
        compiler_params=_cparams(("parallel",)),
        name="router",
    )(logits)


_SCATTER_ROW_BLOCK = 256


def _moe_kernel(h_ref, slot_ref, aff_ref, wg_ref, wu_ref, wd_ref, o_ref, *, cap):
    e = pl.program_id(1)
    bs, l, _ = h_ref.shape
    want = lax.broadcasted_iota(jnp.int32, (cap, l), 0).astype(F32)
    onehots, vals, xes = [], [], []
    for s in range(bs):
        hit = want == slot_ref[s, 0]
        onehots.append(jnp.where(hit, 1.0, 0.0).astype(BF16))
        vals.append(jnp.sum(jnp.where(hit, aff_ref[s, 0], 0.0), axis=-1, keepdims=True))
        xes.append(jnp.dot(onehots[s], h_ref[s], preferred_element_type=F32).astype(BF16))
    xe = jnp.concatenate(xes, axis=0)
    hid = (_silu(jnp.dot(xe, wg_ref[0, 0], preferred_element_type=F32))
           * jnp.dot(xe, wu_ref[0, 0], preferred_element_type=F32)).astype(BF16)
    ye = jnp.dot(hid, wd_ref[0, 0], preferred_element_type=F32) * jnp.concatenate(vals, axis=0)
    tb = min(_SCATTER_ROW_BLOCK, l)

    def scatter(first):
        for s in range(bs):
            yw = ye[s * cap:(s + 1) * cap].astype(BF16)
            for r in range(l // tb):
                rows = slice(r * tb, (r + 1) * tb)
                contrib = lax.dot_general(onehots[s][:, rows], yw, (((0,), (0,)), ((), ())),
                                          preferred_element_type=F32)
                if first:
                    o_ref[s, rows, :] = contrib
                else:
                    o_ref[s, rows, :] += contrib

    pl.when(e == 0)(functools.partial(scatter, True))
    pl.when(e != 0)(functools.partial(scatter, False))


def _moe(h2, slot_t, aff_t, wts, layer, cap, bs):
    b, l, d = h2.shape
    _, ne, _, ff = wts['w_gate'].shape
    row = lambda a: a[:, :ne].reshape(b, ne, 1, l)
    rspec = pl.BlockSpec((bs, 1, 1, l), lambda i, j: (i, j, 0, 0))
    tok = pl.BlockSpec((bs, l, d), lambda i, j: (i, 0, 0))
    wspec = lambda r, c: pl.BlockSpec((1, 1, r, c), lambda i, j: (layer, j, 0, 0))
    return pl.pallas_call(
        functools.partial(_moe_kernel, cap=cap),
        grid=(b // bs, ne),
        in_specs=[tok, rspec, rspec, wspec(d, ff), wspec(d, ff), wspec(ff, d)],
        out_specs=tok,
        out_shape=jax.ShapeDtypeStruct((b, l, d), F32),
        compiler_params=_cparams(("parallel", "arbitrary")),
        name="experts",
    )(h2, row(slot_t), row(aff_t), wts['w_gate'], wts['w_up'], wts['w_down'])


def _final_kernel(x_ref, moe_ref, mod_ref, g_ref, o_ref):
    x = x_ref[0] + mod_ref[0][5:6] * moe_ref[0]
    o_ref[0] = x * lax.rsqrt(jnp.mean(x * x, axis=-1, keepdims=True) + NORM_EPS) * g_ref[...]


def _final(x, moe, mod, g, tm):
    b, l, d = x.shape
    tok = pl.BlockSpec((1, tm, d), lambda i, j: (i, j, 0))
    return pl.pallas_call(
        _final_kernel,
        grid=(b, l // tm),
        in_specs=[tok, tok, pl.BlockSpec((1, 8, d), lambda i, j: (i, 0, 0)),
                  pl.BlockSpec((1, d), lambda i, j: (0, 0))],
        out_specs=tok,
        out_shape=jax.ShapeDtypeStruct((b, l, d), F32),
        compiler_params=_cparams(("parallel", "parallel")),
        name="final_norm",
    )(x, moe, mod, g.reshape(1, d))


def _token_block(l):
    return min(l, 512)


_EXPERT_TOKENS_PER_STEP = 2048


def _samples_per_step(b, l):
    bs = max(1, min(b, _EXPERT_TOKENS_PER_STEP // l))
    while b % bs:
        bs -= 1
    return bs


def kernel(x, c, ctx, c_ctx, ada_w, ada_b, norm1_g, norm2_g, w_in, dn_conv_w, dn_a_log, dn_dt_bias, dn_norm_g, hy_conv_w, hy_conv_b, hy_w1, hy_b1, hy_w2, hy_b2, hy_w3, hy_freq, hy_d_bias, sc_conv_w, w_branch_a, w_branch_b, w_branch_c, w_out, router_w, exp_w_gate, exp_w_up, exp_w_down, final_norm_g):
    depth = ada_w.shape[0]
    b, l, d = x.shape
    lc = ctx.shape[1]
    tm_x, tm_c = _token_block(l), _token_block(lc)
    cap_x = CAPACITY_FACTOR * l // N_EXPERTS
    cap_c = CAPACITY_FACTOR * lc // N_EXPERTS

    rows = -(-(b + 1) // 8) * 8
    cvec = jnp.zeros((rows, d), F32).at[:b].set(c).at[b].set(c_ctx)
    mod_all = _modulation(cvec, ada_w, ada_b).reshape(depth, rows, 6, d)
    mod_all = jnp.pad(mod_all, ((0, 0), (0, 0), (0, 2), (0, 0)))
    mods_x = [mod_all[i, :b] for i in range(depth)]
    mods_c = [jnp.broadcast_to(mod_all[i, b], (b, 8, d)) for i in range(depth)]

    tables_x = _dft_tables(l) + _hy_features(l)
    tables_c = _dft_tables(lc) + _hy_features(lc)
    zero_state = jnp.zeros((b, DN_HEADS, DN_HEAD_DIM, DN_HEAD_DIM), F32)

    w_packed = _pack_w_in(w_in)
    wts = {
        'w_a': w_branch_a.astype(BF16), 'w_b': w_branch_b.astype(BF16), 'w_c': w_branch_c.astype(BF16),
        'w_o': w_out.astype(BF16), 'w_r': jnp.pad(router_w, ((0, 0), (0, 0), (0, LANES - N_EXPERTS))),
        'w_gate': exp_w_gate.astype(BF16), 'w_up': exp_w_up.astype(BF16), 'w_down': exp_w_down.astype(BF16),
    }

    moe_x = moe_c = None
    for i in range(depth):
        lp = {
            'dn_a_log': dn_a_log[i], 'dn_dt_bias': dn_dt_bias[i],
            'dn_norm_g': dn_norm_g[i], 'hy_conv_w': hy_conv_w[i], 'hy_conv_b': hy_conv_b[i],
            'hy_w1': hy_w1[i], 'hy_b1': hy_b1[i], 'hy_w2': hy_w2[i], 'hy_b2': hy_b2[i], 'hy_w3': hy_w3[i],
            'hy_freq': hy_freq[i], 'hy_d_bias': hy_d_bias[i], 'sc_conv_w': sc_conv_w[i],
            'norm2_g': norm2_g[i],
        }
        last = i == depth - 1

        def mixer_inputs(tokens, moe, mods, tm, row_len):
            if moe is None:
                outs = _inproj(tokens, mods[i], norm1_g[i], w_packed, dn_conv_w, i, tm, row_len)
                return tokens, outs
            outs = _inproj(tokens, mods[i], norm1_g[i], w_packed, dn_conv_w, i, tm, row_len,
                           moe=moe, mod_prev=mods[i - 1])
            return outs[-1], outs[:-1]

        def rest_of_layer(tokens, parts, ya, mods, tm, row_len, tables, cap, samples_per_step):
            _, _, hy, sc, gates, _ = parts
            filt = _hy_filter(tokens.shape[1], lp, tables)
            yb = _hyena(hy, lp, filt, tables, row_len)
            tokens, h2, logits = _merge(tokens, ya, yb, sc, gates, mods[i], lp, wts, i, tm, row_len)
            _, _, slot_t, aff_t = _router(logits, cap)
            return tokens, _moe(h2, slot_t, aff_t, wts, i, cap, samples_per_step)

        ctx, parts_c = mixer_inputs(ctx, moe_c, mods_c, tm_c, lc)
        gb_c = _dn_gates(parts_c[5], lp['dn_a_log'], lp['dn_dt_bias'])
        ya_c, s_f, s_b = _deltanet(parts_c[0], parts_c[1], gb_c, lp['dn_norm_g'], zero_state, zero_state)
        if not last:
            ctx, moe_c = rest_of_layer(ctx, parts_c, ya_c, mods_c, tm_c, lc, tables_c, cap_c, _samples_per_step(b, lc))

        x, parts_x = mixer_inputs(x, moe_x, mods_x, tm_x, GRID_W)
        gb_x = _dn_gates(parts_x[5], lp['dn_a_log'], lp['dn_dt_bias'])
        ya_x, _, _ = _deltanet(parts_x[0], parts_x[1], gb_x, lp['dn_norm_g'], s_f, s_b)
        x, moe_x = rest_of_layer(x, parts_x, ya_x, mods_x, tm_x, GRID_W, tables_x, cap_x, _samples_per_step(b, l))

    return _final(x, moe_x, mods_x[depth - 1], final_norm_g, tm_x)
```

```python
import functools
import math

import jax
import jax.numpy as jnp
import numpy as np
from jax import lax
from jax.experimental import pallas as pl
from jax.experimental.pallas import tpu as pltpu

F32 = jnp.float32
BF16 = jnp.bfloat16
HIGHEST = lax.Precision.HIGHEST

GRID_W = 64
DN_HEADS = 4
DN_HEAD_DIM = 128
DN_WIDTH = DN_HEADS * DN_HEAD_DIM
DN_CHUNK = 64
HY_WIDTH = 256
HY_BANDS = 16
HY_TARGET = 1e-2
HY_FAST = 0.3
HY_SLOW = 1.5
SC_WIDTH = 256
N_BRANCH = 3
N_EXPERTS = 16
CAPACITY_FACTOR = 2
NORM_EPS = 1e-6

LANES = 128
VMEM_LIMIT = 56 * 1024 * 1024


def _cparams(sem):
    return pltpu.CompilerParams(dimension_semantics=sem, vmem_limit_bytes=VMEM_LIMIT)


def _mm(a, b):
    return jnp.dot(a.astype(BF16), b.astype(BF16), preferred_element_type=F32)


def _mm_nt(a, b):
    return lax.dot_general(a.astype(BF16), b.astype(BF16), (((1,), (1,)), ((), ())),
                           preferred_element_type=F32)


def _mm_tn(a, b):
    return lax.dot_general(a.astype(BF16), b.astype(BF16), (((0,), (0,)), ((), ())),
                           preferred_element_type=F32)


def _mm_f32(a, b):
    return jnp.dot(a, b, precision=HIGHEST, preferred_element_type=F32)


def _split_bf16(b):
    hi = b.astype(BF16)
    return hi, (b - hi.astype(F32)).astype(BF16)


def _sigmoid(x):
    return 0.5 * jnp.tanh(0.5 * x) + 0.5


def _silu(x):
    return x * _sigmoid(x)


def _softplus(x):
    return jnp.maximum(x, 0.0) + jnp.log(1.0 + jnp.exp(-jnp.abs(x)))


def _conv3(x, w, row_len):
    t = x.shape[0]
    pos = lax.broadcasted_iota(jnp.int32, x.shape, 0) & (row_len - 1)
    prev = jnp.where(pos == 0, 0.0, pltpu.roll(x, 1, axis=0))
    nxt = jnp.where(pos == row_len - 1, 0.0, pltpu.roll(x, t - 1, axis=0))
    return prev * w[0:1] + x * w[1:2] + nxt * w[2:3]


def _mod_kernel(c_ref, w_ref, b_ref, o_ref):
    o_ref[0] = _mm_f32(_silu(c_ref[...]), w_ref[0]) + b_ref[0]


def _modulation(cvec, ada_w, ada_b):
    depth, d, n6 = ada_w.shape
    rows = cvec.shape[0]
    tn = 1536
    return pl.pallas_call(
        _mod_kernel,
        grid=(depth, n6 // tn),
        in_specs=[pl.BlockSpec((rows, d), lambda i, j: (0, 0)),
                  pl.BlockSpec((1, d, tn), lambda i, j: (i, 0, j)),
                  pl.BlockSpec((1, 1, tn), lambda i, j: (i, 0, j))],
        out_specs=pl.BlockSpec((1, rows, tn), lambda i, j: (i, 0, j)),
        out_shape=jax.ShapeDtypeStruct((depth, rows, n6), F32),
        compiler_params=_cparams(("parallel", "parallel")),
        name="modulation",
    )(cvec, ada_w, ada_b.reshape(depth, 1, n6))


_QKV = (0, 3 * DN_WIDTH)
_Z = (_QKV[0] + _QKV[1], DN_WIDTH)
_BA = (_Z[0] + _Z[1], LANES)
_HY = (_BA[0] + _BA[1], 3 * HY_WIDTH)
_SC = (_HY[0] + _HY[1], 3 * SC_WIDTH)
_GATES = (_SC[0] + _SC[1], N_BRANCH * 1024)
N_PACKED = _GATES[0] + _GATES[1]


_PACK_ROWS = 256


def _pack_kernel(w_ref, o_ref):
    used = _BA[0] + 4 * DN_HEADS
    x = w_ref[0]
    o_ref[0, :, :used] = x[:, :used].astype(BF16)
    o_ref[0, :, used:_HY[0]] = jnp.zeros((x.shape[0], _HY[0] - used), BF16)
    o_ref[0, :, _HY[0]:] = x[:, used:].astype(BF16)


def _pack_w_in(w_in):
    depth, d, n_in = w_in.shape
    tr = min(_PACK_ROWS, d)
    return pl.pallas_call(
        _pack_kernel,
        grid=(depth, d // tr),
        in_specs=[pl.BlockSpec((1, tr, n_in), lambda i, j: (i, j, 0))],
        out_specs=pl.BlockSpec((1, tr, N_PACKED), lambda i, j: (i, j, 0)),
        out_shape=jax.ShapeDtypeStruct((depth, d, N_PACKED), BF16),
        compiler_params=_cparams(("parallel", "parallel")),
        name="pack_w_in",
    )(w_in)


def _rms_mod(x, g, scale, shift):
    y = x * lax.rsqrt(jnp.mean(x * x, axis=-1, keepdims=True) + NORM_EPS)
    return (y * g) * (1.0 + scale) + shift


def _inproj_kernel(*refs, has_moe, row_len):
    if has_moe:
        x_ref, moe_ref, modp_ref, mod_ref, g_ref, w_ref, cw_ref = refs[:7]
        outs = refs[7:]
        xo_ref = outs[-1]
        x = x_ref[0] + modp_ref[0][5:6] * moe_ref[0]
        xo_ref[0] = x
    else:
        x_ref, mod_ref, g_ref, w_ref, cw_ref = refs[:5]
        outs = refs[5:]
        x = x_ref[0]
    qkv_ref, z_ref, hy_ref, sc_ref, gates_ref, ba_ref = outs[:6]
    m = mod_ref[0]
    hb = _rms_mod(x, g_ref[...], m[1:2], m[0:1]).astype(BF16)

    def proj(sec, lo=0, width=None):
        a = sec[0] + lo
        wd = sec[1] if width is None else width
        return jnp.dot(hb, w_ref[0, :, a:a + wd], preferred_element_type=F32)

    act = _silu(_conv3(proj(_QKV), cw_ref[0], row_len))
    hd = DN_HEAD_DIM
    for j in range(3 * DN_HEADS):
        t = act[:, j * hd:(j + 1) * hd]
        if j < 2 * DN_HEADS:
            t = t * lax.rsqrt(jnp.sum(t * t, axis=-1, keepdims=True) + 1e-6)
        if j < DN_HEADS:
            t = t * (hd ** -0.5)
        qkv_ref[0, :, j * hd:(j + 1) * hd] = t.astype(BF16)
    z_ref[0] = proj(_Z).astype(BF16)
    hy_ref[0] = proj(_HY).astype(BF16)
    sc_ref[0] = proj(_SC).astype(BF16)
    for j in range(N_BRANCH):
        gates_ref[0, :, j * 1024:(j + 1) * 1024] = proj(_GATES, j * 1024, 1024).astype(BF16)
    ba_ref[0] = proj(_BA)


def _inproj(x, mod, norm_g, w_packed, dn_conv_w, layer, tm, row_len, moe=None, mod_prev=None):
    b, l, d = x.shape
    assert tm % row_len == 0
    has_moe = moe is not None
    tok = pl.BlockSpec((1, tm, d), lambda i, j: (i, j, 0))
    modspec = pl.BlockSpec((1, 8, d), lambda i, j: (i, 0, 0))
    in_specs = [tok]
    args = [x]
    if has_moe:
        in_specs += [tok, modspec]
        args += [moe, mod_prev]
    in_specs += [modspec,
                 pl.BlockSpec((1, d), lambda i, j: (0, 0)),
                 pl.BlockSpec((1, d, N_PACKED), lambda i, j: (layer, 0, 0), pipeline_mode=pl.Buffered(1)),
                 pl.BlockSpec((1,) + dn_conv_w.shape[1:], lambda i, j: (layer, 0, 0))]
    args += [mod, norm_g.reshape(1, d), w_packed, dn_conv_w]
    widths = [(_QKV[1], BF16), (_Z[1], BF16), (_HY[1], BF16), (_SC[1], BF16), (_GATES[1], BF16), (_BA[1], F32)]
    out_specs = [pl.BlockSpec((1, tm, w), lambda i, j: (i, j, 0)) for w, _ in widths]
    out_shape = [jax.ShapeDtypeStruct((b, l, w), dt) for w, dt in widths]
    if has_moe:
        out_specs.append(tok)
        out_shape.append(jax.ShapeDtypeStruct((b, l, d), F32))
    return pl.pallas_call(
        functools.partial(_inproj_kernel, has_moe=has_moe, row_len=row_len),
        grid=(b, l // tm),
        in_specs=in_specs,
        out_specs=out_specs,
        out_shape=out_shape,
        compiler_params=_cparams(("parallel", "parallel")),
        name="inproj",
    )(*args)


_BETA_LANE = 0
_DECAY_LANE = 2 * DN_HEADS
_BWD_DECAY_LANE = _DECAY_LANE + DN_HEADS


def _dn_gates_kernel(ba_ref, alog_ref, dtb_ref, o_ref, *, nc):
    x = ba_ref[0]
    beta = _sigmoid(x)
    g = -jnp.exp(alog_ref[...]) * _softplus(x + dtb_ref[...])
    ii = lax.broadcasted_iota(jnp.int32, (DN_CHUNK, DN_CHUNK), 0)
    jj = lax.broadcasted_iota(jnp.int32, (DN_CHUNK, DN_CHUNK), 1)
    lower = jnp.where(ii >= jj, 1.0, 0.0).astype(F32)
    upper = jnp.where(ii <= jj, 1.0, 0.0).astype(F32)
    lane = lax.broadcasted_iota(jnp.int32, (DN_CHUNK, LANES), 1)
    for c in range(nc):
        rows = slice(c * DN_CHUNK, (c + 1) * DN_CHUNK)
        gch = g[rows]
        acc = jnp.where(lane >= _BWD_DECAY_LANE, _mm_f32(upper, gch), _mm_f32(lower, gch))
        o_ref[0, rows, :] = jnp.where(lane < _DECAY_LANE, beta[rows], acc)


def _dn_gates(ba, a_log, dt_bias):
    b, l, _ = ba.shape
    nc = l // DN_CHUNK
    vec = lambda p: jnp.zeros((1, LANES), F32).at[0, _DECAY_LANE:_DECAY_LANE + 2 * DN_HEADS].set(p.reshape(-1))
    return pl.pallas_call(
        functools.partial(_dn_gates_kernel, nc=nc),
        grid=(b,),
        in_specs=[pl.BlockSpec((1, l, LANES), lambda i: (i, 0, 0)),
                  pl.BlockSpec((1, LANES), lambda i: (0, 0)),
                  pl.BlockSpec((1, LANES), lambda i: (0, 0))],
        out_specs=pl.BlockSpec((1, l, LANES), lambda i: (i, 0, 0)),
        out_shape=jax.ShapeDtypeStruct((b, l, LANES), F32),
        compiler_params=_cparams(("parallel",)),
        name="dn_gates",
    )(ba, vec(a_log), vec(dt_bias))


_GROUP_CHUNKS = 4


def _dn_kernel(qkv_ref, z_ref, gb_ref, gbt_ref, ng_ref, s0f_ref, s0b_ref, ya_ref, sf_ref, sb_ref,
               of_s, ob_s, st_s, *, ngroups, gc):
    c, hd, nh = DN_CHUNK, DN_HEAD_DIM, DN_HEADS
    gt = gc * c
    shift = c.bit_length() - 1

    for h in range(nh):
        st_s[h] = s0f_ref[0, h]
        st_s[nh + h] = s0b_ref[0, h]

    rr = lax.broadcasted_iota(jnp.int32, (c, gt), 0)
    ln = lax.broadcasted_iota(jnp.int32, (c, gt), 1)
    lblk = ln >> shift
    cc = ln & (c - 1)
    eye_p = jnp.where(rr == cc, 1.0, 0.0).astype(F32)
    inside = [(rr >> lv) == (cc >> lv) for lv in range(1, shift + 1)]
    level_masks = [jnp.where(inside[0], 1.0, 0.0).astype(F32)] + [
        jnp.where(jnp.logical_and(inside[j], jnp.logical_not(inside[j - 1])), 1.0, 0.0).astype(F32)
        for j in range(1, shift)]
    same_blk = ((lax.broadcasted_iota(jnp.int32, (gt, gt), 0) >> shift)
                == (lax.broadcasted_iota(jnp.int32, (gt, gt), 1) >> shift))
    blk_mask = jnp.where(same_blk, 1.0, 0.0).astype(BF16)
    rblk = lax.broadcasted_iota(jnp.int32, (gt, 1), 0) >> shift

    def pack_cols(col):
        out = jnp.broadcast_to(col[0:c], (c, gt))
        for i in range(1, gc):
            out = jnp.where(lblk == i, jnp.broadcast_to(col[i * c:(i + 1) * c], (c, gt)), out)
        return out

    def pack_diag(full):
        out = full[0:c]
        for i in range(1, gc):
            out = jnp.where(lblk == i, full[i * c:(i + 1) * c], out)
        return out

    def block_diag(packed_bf16):
        return jnp.concatenate([packed_bf16] * gc, axis=0) * blk_mask

    pair_lane = lax.broadcasted_iota(jnp.int32, (1, 2 * hd), 1)
    first_head = jnp.where(pair_lane < hd, 1.0, 0.0).astype(BF16)
    second_head = jnp.where(pair_lane >= hd, 1.0, 0.0).astype(BF16)

    def pair_diag(pair_bf16):
        return jnp.concatenate([pair_bf16 * first_head, pair_bf16 * second_head], axis=0)

    def dotf(a, b):
        return jnp.dot(a, b, preferred_element_type=F32)

    def prepare(h, d, g):
        r0 = pl.multiple_of(g * gt, gt)
        part = lambda p: qkv_ref[0, pl.ds(r0, gt), (p * nh + h) * hd:(p * nh + h + 1) * hd].astype(F32)
        qg, kg, vg = part(0), part(1), part(2)
        gbg = gb_ref[0, pl.ds(r0, gt), :]
        col = d * nh + h
        beta = gbg[:, _BETA_LANE + col:_BETA_LANE + col + 1]
        gcc = gbg[:, _DECAY_LANE + col:_DECAY_LANE + col + 1]
        gcr = gbt_ref[0, g][_DECAY_LANE + col:_DECAY_LANE + col + 1, :]
        if d == 0:
            incl, strict, last = rr >= cc, rr > cc, c - 1
        else:
            incl, strict, last = rr <= cc, rr < cc, 0
        kq = _mm_nt(jnp.concatenate([kg, qg], axis=0), kg)
        decay = jnp.where(incl, jnp.exp(jnp.where(incl, pack_cols(gcc) - gcr, 0.0)), 0.0)
        m = jnp.where(strict, pack_diag(kq[:gt]) * pack_cols(beta) * decay, 0.0)
        attn = pack_diag(kq[gt:]) * decay
        eg = jnp.exp(gcc)
        glast = jnp.broadcast_to(gcc[last:last + 1], (gt, 1))
        for i in range(1, gc):
            glast = jnp.where(rblk == i, gcc[i * c + last:i * c + last + 1], glast)
        return dict(m=m, attn=attn.astype(BF16), qd=(qg * eg).astype(BF16),
                    kd=(kg * jnp.exp(glast - gcc)).astype(BF16),
                    rhs=jnp.concatenate([vg * beta, kg * (beta * eg)], axis=1).astype(BF16),
                    chunk_decay=[jnp.exp(gcc[i * c + last:i * c + last + 1]) for i in range(gc)])

    def body(j, carry):
        groups = (j, ngroups - 1 - j)
        slots = [(h, d) for h in range(nh) for d in range(2)]
        state = [st_s[d * nh + h] for h, d in slots]
        pre = [prepare(h, d, groups[d]) for h, d in slots]
        ts = [eye_p - p['m'] * level_masks[0] for p in pre]
        for level_mask in level_masks[1:]:
            xs = [dotf((p['m'] * level_mask).astype(BF16), block_diag(t.astype(BF16))) for p, t in zip(pre, ts)]
            ts = [t - dotf(t.astype(BF16), block_diag(x.astype(BF16))) for t, x in zip(ts, xs)]
        uws = [dotf(block_diag(t.astype(BF16)), p['rhs']) for p, t in zip(pre, ts)]
        lanes = lambda parts: jnp.concatenate(parts, axis=1)
        index = {slot: n for n, slot in enumerate(slots)}
        pairs = [(hp, d) for hp in range(nh // 2) for d in range(2)]
        packed = []
        for hp, d in pairs:
            na, nb = index[(2 * hp, d)], index[(2 * hp + 1, d)]
            packed.append(dict(
                s=lanes([state[na], state[nb]]),
                u=lanes([uws[na][:, :hd], uws[nb][:, :hd]]),
                w=lanes([uws[na][:, hd:], uws[nb][:, hd:]]).astype(BF16),
                qd=lanes([pre[na]['qd'], pre[nb]['qd']]), a=pre[na], b=pre[nb]))
        outs = [[None] * gc for _ in pairs]
        for step in range(gc):
            for n, (hp, d) in enumerate(pairs):
                i = step if d == 0 else gc - 1 - step
                rows = slice(i * c, (i + 1) * c)
                q = packed[n]
                wq = jnp.concatenate([q['w'][rows], q['qd'][rows]], axis=0)
                a = dotf(wq, pair_diag(q['s'].astype(BF16)))
                v_new = (q['u'][rows] - a[:c]).astype(BF16)
                lhs = jnp.concatenate([lanes([q['a']['attn'][:, rows], q['b']['attn'][:, rows]]),
                                       lanes([q['a']['kd'][rows].T, q['b']['kd'][rows].T])], axis=0)
                av = dotf(lhs, pair_diag(v_new))
                outs[n][i] = a[c:] + av[:c]
                decay = lanes([jnp.broadcast_to(q['a']['chunk_decay'][i], (1, hd)),
                               jnp.broadcast_to(q['b']['chunk_decay'][i], (1, hd))])
                q['s'] = q['s'] * decay + av[c:]
        for n, (hp, d) in enumerate(pairs):
            st_s[d * nh + 2 * hp] = packed[n]['s'][:, :hd]
            st_s[d * nh + 2 * hp + 1] = packed[n]['s'][:, hd:]
            r0 = pl.multiple_of(groups[d] * gt, gt)
            (of_s, ob_s)[d][pl.ds(r0, gt), 2 * hp * hd:(2 * hp + 2) * hd] = jnp.concatenate(outs[n], axis=0)
        return carry

    lax.fori_loop(0, ngroups, body, 0)
    for h in range(nh):
        sf_ref[0, h] = st_s[h]
        sb_ref[0, h] = st_s[nh + h]
        hs = slice(h * hd, (h + 1) * hd)
        o = of_s[:, hs] + ob_s[:, hs]
        y = o * lax.rsqrt(jnp.mean(o * o, axis=-1, keepdims=True) + NORM_EPS) * ng_ref[...]
        ya_ref[0, :, hs] = (y * _silu(z_ref[0, :, hs].astype(F32))).astype(BF16)


def _deltanet(qkv, z, gb, norm_g, s0f, s0b):
    b, l, _ = qkv.shape
    c, hd, nh = DN_CHUNK, DN_HEAD_DIM, DN_HEADS
    gc = min(_GROUP_CHUNKS, l // c)
    gt = gc * c
    ngroups = l // gt
    nlanes = 4 * nh
    gbt = jnp.transpose(gb[:, :, :nlanes].reshape(b, ngroups, gt, nlanes), (0, 1, 3, 2))
    st_blk = pl.BlockSpec((1, nh, hd, hd), lambda i: (i, 0, 0, 0))
    seq_blk = lambda w: pl.BlockSpec((1, l, w), lambda i: (i, 0, 0))
    return pl.pallas_call(
        functools.partial(_dn_kernel, ngroups=ngroups, gc=gc),
        grid=(b,),
        in_specs=[seq_blk(3 * DN_WIDTH), seq_blk(DN_WIDTH), seq_blk(LANES),
                  pl.BlockSpec((1, ngroups, nlanes, gt), lambda i: (i, 0, 0, 0)),
                  pl.BlockSpec((1, hd), lambda i: (0, 0)),
                  st_blk, st_blk],
        out_specs=[seq_blk(DN_WIDTH), st_blk, st_blk],
        out_shape=[jax.ShapeDtypeStruct((b, l, DN_WIDTH), BF16),
                   jax.ShapeDtypeStruct((b, nh, hd, hd), F32),
                   jax.ShapeDtypeStruct((b, nh, hd, hd), F32)],
        scratch_shapes=[pltpu.VMEM((l, DN_WIDTH), F32),
                        pltpu.VMEM((l, DN_WIDTH), F32),
                        pltpu.VMEM((2 * nh, hd, hd), F32)],
        compiler_params=_cparams(("parallel",)),
        name="deltanet",
    )(qkv, z, gb, gbt, norm_g.reshape(1, hd), s0f, s0b)


def _dft_tables(l):
    n = 2 * l
    nb = GRID_W
    f = jnp.arange(l, dtype=jnp.int32)[:, None]
    angle = lambda k: (k % n).astype(F32) * (2.0 * math.pi / n)
    ang_a = angle(f * (nb * jnp.arange(l // nb, dtype=jnp.int32))[None, :])[:, :, None]
    ang_b = angle(f * jnp.arange(nb, dtype=jnp.int32)[None, :])[:, None, :]
    ca, sa, cb, sb = jnp.cos(ang_a), jnp.sin(ang_a), jnp.cos(ang_b), jnp.sin(ang_b)
    return ((ca * cb - sa * sb).reshape(l, l).astype(BF16), (sa * cb + ca * sb).reshape(l, l).astype(BF16))


def _hy_features(l):
    pos = jnp.arange(l, dtype=F32)
    t = (pos / max(l - 1, 1))[:, None]
    bands = jnp.linspace(1e-4, HY_BANDS - 1, HY_BANDS, dtype=F32)
    ang = (2.0 * math.pi / l) * pos[:, None] * bands[None, :]
    feats = jnp.concatenate([t, jnp.cos(ang), -jnp.sin(ang)], axis=-1)
    feats = jnp.pad(feats, ((0, 0), (0, LANES - feats.shape[1])))
    deltas = jnp.abs(jnp.linspace(math.log(HY_TARGET) / HY_SLOW, math.log(HY_TARGET) / HY_FAST,
                                  HY_WIDTH, dtype=F32))
    return feats, t, deltas[None, :]


def _hy_filter_kernel(feats_ref, t_ref, deltas_ref, w1_ref, b1_ref, w2_ref, b2_ref, w3_ref, freq_ref,
                      cos_ref, sin_ref, ka_ref, kb_ref, kn_ref):
    l = feats_ref.shape[0]
    n = 2 * l
    freq = freq_ref[...]
    hh = jnp.sin(freq * (_mm_f32(feats_ref[...], w1_ref[...]) + b1_ref[...]))
    hh = jnp.sin(freq * (_mm_f32(hh, w2_ref[...]) + b2_ref[...]))
    hh = _mm_f32(hh, w3_ref[...])
    window = jnp.exp(-t_ref[...] * deltas_ref[...])
    h_fwd = hh[:, :HY_WIDTH] * window
    h_bwd = hh[:, HY_WIDTH:] * window
    row = lax.broadcasted_iota(jnp.int32, (l, HY_WIDTH), 0)
    h_bwd = jnp.where(row == 0, 0.0, h_bwd)
    hs = h_fwd + h_bwd
    hd = h_fwd - h_bwd
    hs_hi, hs_lo = _split_bf16(hs)
    hd_hi, hd_lo = _split_bf16(hd)
    blk = min(_DFT_ROW_BLOCK, l)
    brow = lax.broadcasted_iota(jnp.int32, (blk, HY_WIDTH), 0)
    for i in range(l // blk):
        rows = slice(i * blk, (i + 1) * blk)
        cos_b = cos_ref[rows, :]
        sin_b = sin_ref[rows, :]
        ka = jnp.dot(cos_b, hs_hi, preferred_element_type=F32) + jnp.dot(cos_b, hs_lo, preferred_element_type=F32)
        kb = jnp.dot(sin_b, hd_hi, preferred_element_type=F32) + jnp.dot(sin_b, hd_lo, preferred_element_type=F32)
        ka_ref[rows, :] = ka * (jnp.where(brow == 0, 1.0 / n, 2.0 / n) if i == 0 else 2.0 / n)
        kb_ref[rows, :] = kb * (2.0 / n)
    nyq = jnp.sum(jnp.where((row & 1) == 0, hs, -hs), axis=0, keepdims=True) * (1.0 / n)
    kn_ref[...] = jnp.broadcast_to(nyq, kn_ref.shape)


def _hy_filter(l, lp, tables):
    cos_t, sin_t, feats, t, deltas = tables
    w1 = jnp.pad(lp['hy_w1'], ((0, LANES - lp['hy_w1'].shape[0]), (0, 0)))
    row = lambda v: v.reshape(1, -1)
    args = (feats, t, deltas, w1, row(lp['hy_b1']), lp['hy_w2'], row(lp['hy_b2']), lp['hy_w3'],
            row(lp['hy_freq']), cos_t, sin_t)
    return pl.pallas_call(
        _hy_filter_kernel,
        out_shape=[jax.ShapeDtypeStruct((l, HY_WIDTH), F32),
                   jax.ShapeDtypeStruct((l, HY_WIDTH), F32),
                   jax.ShapeDtypeStruct((8, HY_WIDTH), F32)],
        compiler_params=pltpu.CompilerParams(vmem_limit_bytes=VMEM_LIMIT),
        name="hyena_filter",
    )(*args)


_DFT_ROW_BLOCK = 256


def _hyena_kernel(hy_ref, cw_ref, cb_ref, db_ref, cos_ref, sin_ref, ka_ref, kb_ref, kn_ref, o_ref, *, row_len):
    l = hy_ref.shape[1]
    u = _conv3(hy_ref[0].astype(F32), cw_ref[...], row_len) + cb_ref[...]
    x0 = u[:, :HY_WIDTH]
    z = u[:, HY_WIDTH:2 * HY_WIDTH] * u[:, 2 * HY_WIDTH:]
    zb = z.astype(BF16)
    blk = min(_DFT_ROW_BLOCK, l)
    blocks = [slice(i * blk, (i + 1) * blk) for i in range(l // blk)]
    ya, yb = [], []
    for rows in blocks:
        xa = jnp.dot(cos_ref[rows, :], zb, preferred_element_type=F32)
        xb = jnp.dot(sin_ref[rows, :], zb, preferred_element_type=F32)
        ka = ka_ref[rows, :]
        kb = kb_ref[rows, :]
        ya.append((xa * ka - xb * kb).astype(BF16))
        yb.append((xa * kb + xb * ka).astype(BF16))
    ya = jnp.concatenate(ya, axis=0)
    yb = jnp.concatenate(yb, axis=0)
    even = (lax.broadcasted_iota(jnp.int32, (blk, HY_WIDTH), 0) & 1) == 0
    sign = jnp.where((lax.broadcasted_iota(jnp.int32, (l, HY_WIDTH), 0) & 1) == 0, 1.0, -1.0)
    y_nyq = jnp.sum(z * sign, axis=0, keepdims=True) * kn_ref[0:1]
    for rows in blocks:
        y = (jnp.dot(cos_ref[rows, :], ya, preferred_element_type=F32)
             + jnp.dot(sin_ref[rows, :], yb, preferred_element_type=F32))
        y = y + jnp.where(even, y_nyq, -y_nyq)
        o_ref[0, rows, :] = (x0[rows] * (y + z[rows] * db_ref[...])).astype(BF16)


def _hyena(hy, lp, filt, tables, row_len):
    b, l, w3 = hy.shape
    ka, kb, kn = filt
    cos_t, sin_t = tables[0], tables[1]
    const = lambda shape: pl.BlockSpec(shape, lambda i: (0,) * len(shape), pipeline_mode=pl.Buffered(1))
    return pl.pallas_call(
        functools.partial(_hyena_kernel, row_len=row_len),
        grid=(b,),
        in_specs=[pl.BlockSpec((1, l, w3), lambda i: (i, 0, 0)),
                  const((3, w3)), const((1, w3)), const((1, HY_WIDTH)),
                  const((l, l)), const((l, l)),
                  const((l, HY_WIDTH)), const((l, HY_WIDTH)), const((8, HY_WIDTH))],
        out_specs=pl.BlockSpec((1, l, HY_WIDTH), lambda i: (i, 0, 0)),
        out_shape=jax.ShapeDtypeStruct((b, l, HY_WIDTH), BF16),
        compiler_params=_cparams(("parallel",)),
        name="hyena",
    )(hy, lp['hy_conv_w'], lp['hy_conv_b'].reshape(1, w3), lp['hy_d_bias'].reshape(1, HY_WIDTH),
      cos_t, sin_t, ka, kb, kn)


def _merge_kernel(x_ref, ya_ref, yb_ref, sc_ref, gates_ref, mod_ref, n2_ref, scw_ref, wa_ref, wb_ref,
                  wc_ref, wo_ref, wr_ref, xo_ref, h2_ref, lg_ref, *, row_len, parts):
    tm, d = x_ref.shape[1:]
    m = mod_ref[0]
    w_hi, w_lo = _split_bf16(wr_ref[0])
    for part in range(parts):
        rows = slice(part * (tm // parts), (part + 1) * (tm // parts))
        sc = sc_ref[0, rows, :].astype(F32)
        yc = sc[:, :SC_WIDTH] * _conv3(sc[:, SC_WIDTH:2 * SC_WIDTH] * sc[:, 2 * SC_WIDTH:], scw_ref[...], row_len)
        gate = lambda j: _sigmoid(gates_ref[0, rows, j * d:(j + 1) * d].astype(F32))
        merged = (gate(0) * jnp.dot(ya_ref[0, rows, :], wa_ref[0], preferred_element_type=F32)
                  + gate(1) * jnp.dot(yb_ref[0, rows, :], wb_ref[0], preferred_element_type=F32)
                  + gate(2) * _mm(yc, wc_ref[0]))
        x = x_ref[0, rows, :] + m[2:3] * _mm(merged, wo_ref[0])
        xo_ref[0, rows, :] = x
        h2 = _rms_mod(x, n2_ref[...], m[4:5], m[3:4])
        h2_ref[0, rows, :] = h2.astype(BF16)
        h_hi, h_lo = _split_bf16(h2)
        lg_ref[0, rows, :] = (jnp.dot(h_hi, w_hi, preferred_element_type=F32)
                              + jnp.dot(h_lo, w_hi, preferred_element_type=F32)
                              + jnp.dot(h_hi, w_lo, preferred_element_type=F32))


_MERGE_PART_ROWS = 256


def _merge(x, ya, yb, sc, gates, mod, lp, wts, layer, tm, row_len):
    b, l, d = x.shape
    parts = max(1, tm // max(row_len, _MERGE_PART_ROWS))
    tok = lambda w: pl.BlockSpec((1, tm, w), lambda i, j: (i, j, 0))
    const = lambda shape: pl.BlockSpec(shape, lambda i, j: (0,) * len(shape), pipeline_mode=pl.Buffered(1))
    per_layer = lambda r, c: pl.BlockSpec((1, r, c), lambda i, j: (layer, 0, 0), pipeline_mode=pl.Buffered(1))
    return pl.pallas_call(
        functools.partial(_merge_kernel, row_len=row_len, parts=parts),
        grid=(b, l // tm),
        in_specs=[tok(d), tok(DN_WIDTH), tok(HY_WIDTH), tok(3 * SC_WIDTH), tok(N_BRANCH * d),
                  pl.BlockSpec((1, 8, d), lambda i, j: (i, 0, 0)),
                  const((1, d)), const((3, SC_WIDTH)),
                  per_layer(DN_WIDTH, d), per_layer(HY_WIDTH, d), per_layer(SC_WIDTH, d), per_layer(d, d),
                  per_layer(d, LANES)],
        out_specs=[tok(d), tok(d), tok(LANES)],
        out_shape=[jax.ShapeDtypeStruct((b, l, d), F32),
                   jax.ShapeDtypeStruct((b, l, d), BF16),
                   jax.ShapeDtypeStruct((b, l, LANES), F32)],
        compiler_params=_cparams(("parallel", "parallel")),
        name="merge",
    )(x, ya, yb, sc, gates, mod, lp['norm2_g'].reshape(1, d), lp['sc_conv_w'],
      wts['w_a'], wts['w_b'], wts['w_c'], wts['w_o'], wts['w_r'])


_PREFIX_BLOCK = 256


def _router_kernel(lg_ref, slot_ref, aff_ref, slott_ref, afft_ref, *, cap):
    l = lg_ref.shape[1]
    lane = lax.broadcasted_iota(jnp.int32, (l, LANES), 1)
    valid = lane < N_EXPERTS
    lg = jnp.where(valid, lg_ref[0], -jnp.inf)
    e = jnp.where(valid, jnp.exp(lg - jnp.max(lg, axis=-1, keepdims=True)), 0.0)
    aff = e / jnp.sum(e, axis=-1, keepdims=True)

    def count(mask):
        return jnp.sum(jnp.where(mask, 1.0, 0.0), axis=0, keepdims=True)

    aff_t = aff.T
    dense = aff_t[:N_EXPERTS]

    def bisect(i, lo):
        cand = lo | jnp.left_shift(jnp.int32(1), 30 - i)
        n_ge = jnp.sum(jnp.where(dense >= pltpu.bitcast(cand, F32)[:, :1], 1.0, 0.0), axis=-1, keepdims=True)
        return jnp.where(n_ge >= cap, cand, lo)

    thr_t = pltpu.bitcast(lax.fori_loop(0, 31, bisect, jnp.zeros((N_EXPERTS, LANES), jnp.int32)), F32)
    on_diag = (lax.broadcasted_iota(jnp.int32, (N_EXPERTS, LANES), 0)
               == lax.broadcasted_iota(jnp.int32, (N_EXPERTS, LANES), 1))
    thr = jnp.sum(jnp.where(on_diag, thr_t, 0.0), axis=0, keepdims=True)
    above = aff > thr
    tie = aff == thr
    need = cap - count(above)

    pb = _PREFIX_BLOCK if l % _PREFIX_BLOCK == 0 else l
    ii = lax.broadcasted_iota(jnp.int32, (pb, pb), 0)
    jj = lax.broadcasted_iota(jnp.int32, (pb, pb), 1)
    before = jnp.where(ii > jj, 1.0, 0.0).astype(BF16)

    def prefix(mask):
        vals = jnp.where(mask, 1.0, 0.0)
        out = []
        offset = jnp.zeros((1, LANES), F32)
        for r in range(l // pb):
            blk = vals[r * pb:(r + 1) * pb]
            out.append(jnp.dot(before, blk.astype(BF16), preferred_element_type=F32) + offset)
            offset = offset + jnp.sum(blk, axis=0, keepdims=True)
        return jnp.concatenate(out, axis=0)

    chosen = jnp.logical_or(above, jnp.logical_and(tie, prefix(tie) < need))
    chosen = jnp.logical_and(chosen, valid)
    slot = jnp.where(chosen, prefix(chosen), -1.0)
    slot_ref[0] = slot
    aff_ref[0] = aff
    slott_ref[0] = slot.T
    afft_ref[0] = aff_t


def _router(logits, cap):
    b, l, _ = logits.shape
    blk = pl.BlockSpec((1, l, LANES), lambda i: (i, 0, 0))
    blk_t = pl.BlockSpec((1, LANES, l), lambda i: (i, 0, 0))
    return pl.pallas_call(
        functools.partial(_router_kernel, cap=cap),
        grid=(b,),
        in_specs=[blk],
        out_specs=[blk, blk, blk_t, blk_t],
        out_shape=[jax.ShapeDtypeStruct((b, l, LANES), F32), jax.ShapeDtypeStruct((b, l, LANES), F32),
                   jax.ShapeDtypeStruct((b, LANES, l), F32), jax.ShapeDtypeStruct((b, LANES, l), F32)],
        compiler_params=_cparams(("parallel",)),
        name="router",
    )(logits)


_SCATTER_ROW_BLOCK = 256


def _moe_kernel(h_ref, slot_ref, aff_ref, wg_ref, wu_ref, wd_ref, o_ref, *, cap):
    e = pl.program_id(1)
    bs, l, _ = h_ref.shape
    want = lax.broadcasted_iota(jnp.int32, (cap, l), 0).astype(F32)
    onehots, vals, xes = [], [], []
    for s in range(bs):
        hit = want == slot_ref[s, 0]
        onehots.append(jnp.where(hit, 1.0, 0.0).astype(BF16))
        vals.append(jnp.sum(jnp.where(hit, aff_ref[s, 0], 0.0), axis=-1, keepdims=True))
        xes.append(jnp.dot(onehots[s], h_ref[s], preferred_element_type=F32).astype(BF16))
    xe = jnp.concatenate(xes, axis=0)
    hid = (_silu(jnp.dot(xe, wg_ref[0, 0].astype(BF16), preferred_element_type=F32))
           * jnp.dot(xe, wu_ref[0, 0].astype(BF16), preferred_element_type=F32)).astype(BF16)
    ye = jnp.dot(hid, wd_ref[0, 0].astype(BF16), preferred_element_type=F32) * jnp.concatenate(vals, axis=0)
    tb = min(_SCATTER_ROW_BLOCK, l)

    def scatter(first):
        for s in range(bs):
            yw = ye[s * cap:(s + 1) * cap].astype(BF16)
            for r in range(l // tb):
                rows = slice(r * tb, (r + 1) * tb)
                contrib = lax.dot_general(onehots[s][:, rows], yw, (((0,), (0,)), ((), ())),
                                          preferred_element_type=F32)
                if first:
                    o_ref[s, rows, :] = contrib
                else:
                    o_ref[s, rows, :] += contrib

    pl.when(e == 0)(functools.partial(scatter, True))
    pl.when(e != 0)(functools.partial(scatter, False))


def _moe(h2, slot_t, aff_t, wts, layer, cap, bs):
    b, l, d = h2.shape
    _, ne, _, ff = wts['w_gate'].shape
    row = lambda a: a[:, :ne].reshape(b, ne, 1, l)
    rspec = pl.BlockSpec((bs, 1, 1, l), lambda i, j: (i, j, 0, 0))
    tok = pl.BlockSpec((bs, l, d), lambda i, j: (i, 0, 0))
    wspec = lambda r, c: pl.BlockSpec((1, 1, r, c), lambda i, j: (layer, j, 0, 0))
    return pl.pallas_call(
        functools.partial(_moe_kernel, cap=cap),
        grid=(b // bs, ne),
        in_specs=[tok, rspec, rspec, wspec(d, ff), wspec(d, ff), wspec(ff, d)],
        out_specs=tok,
        out_shape=jax.ShapeDtypeStruct((b, l, d), F32),
        compiler_params=_cparams(("parallel", "arbitrary")),
        name="experts",
    )(h2, row(slot_t), row(aff_t), wts['w_gate'], wts['w_up'], wts['w_down'])


def _final_kernel(x_ref, moe_ref, mod_ref, g_ref, o_ref):
    x = x_ref[0] + mod_ref[0][5:6] * moe_ref[0]
    o_ref[0] = x * lax.rsqrt(jnp.mean(x * x, axis=-1, keepdims=True) + NORM_EPS) * g_ref[...]


def _final(x, moe, mod, g, tm):
    b, l, d = x.shape
    tok = pl.BlockSpec((1, tm, d), lambda i, j: (i, j, 0))
    return pl.pallas_call(
        _final_kernel,
        grid=(b, l // tm),
        in_specs=[tok, tok, pl.BlockSpec((1, 8, d), lambda i, j: (i, 0, 0)),
                  pl.BlockSpec((1, d), lambda i, j: (0, 0))],
        out_specs=tok,
        out_shape=jax.ShapeDtypeStruct((b, l, d), F32),
        compiler_params=_cparams(("parallel", "parallel")),
        name="final_norm",
    )(x, moe, mod, g.reshape(1, d))


def _token_block(l):
    return min(l, 512)


_EXPERT_TOKENS_PER_STEP = 2048


def _samples_per_step(b, l):
    bs = max(1, min(b, _EXPERT_TOKENS_PER_STEP // l))
    while b % bs:
        bs -= 1
    return bs


def kernel(x, c, ctx, c_ctx, ada_w, ada_b, norm1_g, norm2_g, w_in, dn_conv_w, dn_a_log, dn_dt_bias, dn_norm_g, hy_conv_w, hy_conv_b, hy_w1, hy_b1, hy_w2, hy_b2, hy_w3, hy_freq, hy_d_bias, sc_conv_w, w_branch_a, w_branch_b, w_branch_c, w_out, router_w, exp_w_gate, exp_w_up, exp_w_down, final_norm_g):
    depth = ada_w.shape[0]
    b, l, d = x.shape
    lc = ctx.shape[1]
    tm_x, tm_c = _token_block(l), _token_block(lc)
    cap_x = CAPACITY_FACTOR * l // N_EXPERTS
    cap_c = CAPACITY_FACTOR * lc // N_EXPERTS

    rows = -(-(b + 1) // 8) * 8
    cvec = jnp.zeros((rows, d), F32).at[:b].set(c).at[b].set(c_ctx)
    mod_all = _modulation(cvec, ada_w, ada_b).reshape(depth, rows, 6, d)
    mod_all = jnp.pad(mod_all, ((0, 0), (0, 0), (0, 2), (0, 0)))
    mods_x = [mod_all[i, :b] for i in range(depth)]
    mods_c = [jnp.broadcast_to(mod_all[i, b], (b, 8, d)) for i in range(depth)]

    tables_x = _dft_tables(l) + _hy_features(l)
    tables_c = _dft_tables(lc) + _hy_features(lc)
    zero_state = jnp.zeros((b, DN_HEADS, DN_HEAD_DIM, DN_HEAD_DIM), F32)

    w_packed = _pack_w_in(w_in)
    wts = {
        'w_a': w_branch_a.astype(BF16), 'w_b': w_branch_b.astype(BF16), 'w_c': w_branch_c.astype(BF16),
        'w_o': w_out.astype(BF16), 'w_r': jnp.pad(router_w, ((0, 0), (0, 0), (0, LANES - N_EXPERTS))),
        'w_gate': exp_w_gate, 'w_up': exp_w_up, 'w_down': exp_w_down,
    }

    moe_x = moe_c = None
    for i in range(depth):
        lp = {
            'dn_a_log': dn_a_log[i], 'dn_dt_bias': dn_dt_bias[i],
            'dn_norm_g': dn_norm_g[i], 'hy_conv_w': hy_conv_w[i], 'hy_conv_b': hy_conv_b[i],
            'hy_w1': hy_w1[i], 'hy_b1': hy_b1[i], 'hy_w2': hy_w2[i], 'hy_b2': hy_b2[i], 'hy_w3': hy_w3[i],
            'hy_freq': hy_freq[i], 'hy_d_bias': hy_d_bias[i], 'sc_conv_w': sc_conv_w[i],
            'norm2_g': norm2_g[i],
        }
        last = i == depth - 1

        def mixer_inputs(tokens, moe, mods, tm, row_len):
            if moe is None:
                outs = _inproj(tokens, mods[i], norm1_g[i], w_packed, dn_conv_w, i, tm, row_len)
                return tokens, outs
            outs = _inproj(tokens, mods[i], norm1_g[i], w_packed, dn_conv_w, i, tm, row_len,
                           moe=moe, mod_prev=mods[i - 1])
            return outs[-1], outs[:-1]

        def rest_of_layer(tokens, parts, ya, mods, tm, row_len, tables, cap, samples_per_step):
            _, _, hy, sc, gates, _ = parts
            filt = _hy_filter(tokens.shape[1], lp, tables)
            yb = _hyena(hy, lp, filt, tables, row_len)
            tokens, h2, logits = _merge(tokens, ya, yb, sc, gates, mods[i], lp, wts, i, tm, row_len)
            _, _, slot_t, aff_t = _router(logits, cap)
            return tokens, _moe(h2, slot_t, aff_t, wts, i, cap, samples_per_step)

        ctx, parts_c = mixer_inputs(ctx, moe_c, mods_c, tm_c, lc)
        gb_c = _dn_gates(parts_c[5], lp['dn_a_log'], lp['dn_dt_bias'])
        ya_c, s_f, s_b = _deltanet(parts_c[0], parts_c[1], gb_c, lp['dn_norm_g'], zero_state, zero_state)
        if not last:
            ctx, moe_c = rest_of_layer(ctx, parts_c, ya_c, mods_c, tm_c, lc, tables_c, cap_c, _samples_per_step(b, lc))

        x, parts_x = mixer_inputs(x, moe_x, mods_x, tm_x, GRID_W)
        gb_x = _dn_gates(parts_x[5], lp['dn_a_log'], lp['dn_dt_bias'])
        ya_x, _, _ = _deltanet(parts_x[0], parts_x[1], gb_x, lp['dn_norm_g'], s_f, s_b)
        x, moe_x = rest_of_layer(x, parts_x, ya_x, mods_x, tm_x, GRID_W, tables_x, cap_x, _samples_per_step(b, l))

    return _final(x, moe_x, mods_x[depth - 1], final_norm_g, tm_x)
```

```python
import functools
import math

import jax
import jax.numpy as jnp
from jax import lax
from jax.experimental import pallas as pl
from jax.experimental.pallas import tpu as pltpu

F32 = jnp.float32
BF16 = jnp.bfloat16
HIGHEST = lax.Precision.HIGHEST

GRID_W = 64
DN_HEADS = 4
DN_HEAD_DIM = 128
DN_WIDTH = DN_HEADS * DN_HEAD_DIM
DN_CHUNK = 64
HY_WIDTH = 256
HY_BANDS = 16
HY_TARGET = 1e-2
HY_FAST = 0.3
HY_SLOW = 1.5
SC_WIDTH = 256
N_BRANCH = 3
N_EXPERTS = 16
CAPACITY_FACTOR = 2
NORM_EPS = 1e-6

LANES = 128
VMEM_LIMIT = 56 * 1024 * 1024


def _cparams(sem):
    return pltpu.CompilerParams(dimension_semantics=sem, vmem_limit_bytes=VMEM_LIMIT)


def _mm(a, b):
    return jnp.dot(a.astype(BF16), b.astype(BF16), preferred_element_type=F32)


def _mm_nt(a, b):
    return lax.dot_general(a.astype(BF16), b.astype(BF16), (((1,), (1,)), ((), ())),
                           preferred_element_type=F32)


def _mm_f32(a, b):
    return jnp.dot(a, b, precision=HIGHEST, preferred_element_type=F32)


def _split_bf16(b):
    hi = b.astype(BF16)
    return hi, (b - hi.astype(F32)).astype(BF16)


def _sigmoid(x):
    return 0.5 * jnp.tanh(0.5 * x) + 0.5


def _silu(x):
    return x * _sigmoid(x)


def _softplus(x):
    return jnp.maximum(x, 0.0) + jnp.log(1.0 + jnp.exp(-jnp.abs(x)))


def _conv3(x, w, row_len):
    t = x.shape[0]
    pos = lax.broadcasted_iota(jnp.int32, x.shape, 0) & (row_len - 1)
    prev = jnp.where(pos == 0, 0.0, pltpu.roll(x, 1, axis=0))
    nxt = jnp.where(pos == row_len - 1, 0.0, pltpu.roll(x, t - 1, axis=0))
    return prev * w[0:1] + x * w[1:2] + nxt * w[2:3]


def _mod_kernel(c_ref, w_ref, b_ref, o_ref):
    o_ref[0] = _mm_f32(_silu(c_ref[...]), w_ref[0]) + b_ref[0]


def _modulation(cvec, ada_w, ada_b):
    depth, d, n6 = ada_w.shape
    rows = cvec.shape[0]
    tn = 1536
    return pl.pallas_call(
        _mod_kernel,
        grid=(depth, n6 // tn),
        in_specs=[pl.BlockSpec((rows, d), lambda i, j: (0, 0)),
                  pl.BlockSpec((1, d, tn), lambda i, j: (i, 0, j)),
                  pl.BlockSpec((1, 1, tn), lambda i, j: (i, 0, j))],
        out_specs=pl.BlockSpec((1, rows, tn), lambda i, j: (i, 0, j)),
        out_shape=jax.ShapeDtypeStruct((depth, rows, n6), F32),
        compiler_params=_cparams(("parallel", "parallel")),
        name="modulation",
    )(cvec, ada_w, ada_b.reshape(depth, 1, n6))


_QKV = (0, 3 * DN_WIDTH)
_Z = (_QKV[0] + _QKV[1], DN_WIDTH)
_BA = (_Z[0] + _Z[1], LANES)
_HY = (_BA[0] + _BA[1], 3 * HY_WIDTH)
_SC = (_HY[0] + _HY[1], 3 * SC_WIDTH)
_GATES = (_SC[0] + _SC[1], N_BRANCH * 1024)
N_PACKED = _GATES[0] + _GATES[1]


_PACK_ROWS = 256


def _pack_kernel(w_ref, o_ref):
    used = _BA[0] + 4 * DN_HEADS
    x = w_ref[0]
    o_ref[0, :, :used] = x[:, :used].astype(BF16)
    o_ref[0, :, used:_HY[0]] = jnp.zeros((x.shape[0], _HY[0] - used), BF16)
    o_ref[0, :, _HY[0]:] = x[:, used:].astype(BF16)


def _pack_w_in(w_in):
    depth, d, n_in = w_in.shape
    tr = min(_PACK_ROWS, d)
    return pl.pallas_call(
        _pack_kernel,
        grid=(depth, d // tr),
        in_specs=[pl.BlockSpec((1, tr, n_in), lambda i, j: (i, j, 0))],
        out_specs=pl.BlockSpec((1, tr, N_PACKED), lambda i, j: (i, j, 0)),
        out_shape=jax.ShapeDtypeStruct((depth, d, N_PACKED), BF16),
        compiler_params=_cparams(("parallel", "parallel")),
        name="pack_w_in",
    )(w_in)


def _rms_mod(x, g, scale, shift):
    y = x * lax.rsqrt(jnp.mean(x * x, axis=-1, keepdims=True) + NORM_EPS)
    return (y * g) * (1.0 + scale) + shift


def _inproj_kernel(*refs, has_moe, row_len):
    if has_moe:
        x_ref, moe_ref, modp_ref, mod_ref, g_ref, w_ref, cw_ref = refs[:7]
        outs = refs[7:]
        xo_ref = outs[-1]
        x = x_ref[0] + modp_ref[0][5:6] * moe_ref[0]
        xo_ref[0] = x
    else:
        x_ref, mod_ref, g_ref, w_ref, cw_ref = refs[:5]
        outs = refs[5:]
        x = x_ref[0]
    qkv_ref, z_ref, hy_ref, sc_ref, gates_ref, ba_ref = outs[:6]
    m = mod_ref[0]
    hb = _rms_mod(x, g_ref[...], m[1:2], m[0:1]).astype(BF16)

    def proj(sec, lo=0, width=None):
        a = sec[0] + lo
        wd = sec[1] if width is None else width
        return jnp.dot(hb, w_ref[0, :, a:a + wd], preferred_element_type=F32)

    act = _silu(_conv3(proj(_QKV), cw_ref[0], row_len))
    hd = DN_HEAD_DIM
    for j in range(3 * DN_HEADS):
        t = act[:, j * hd:(j + 1) * hd]
        if j < 2 * DN_HEADS:
            t = t * lax.rsqrt(jnp.sum(t * t, axis=-1, keepdims=True) + 1e-6)
        if j < DN_HEADS:
            t = t * (hd ** -0.5)
        qkv_ref[0, :, j * hd:(j + 1) * hd] = t.astype(BF16)
    z_ref[0] = proj(_Z).astype(BF16)
    hy_ref[0] = proj(_HY).astype(BF16)
    sc_ref[0] = proj(_SC).astype(BF16)
    for j in range(N_BRANCH):
        gates_ref[0, :, j * 1024:(j + 1) * 1024] = proj(_GATES, j * 1024, 1024).astype(BF16)
    ba_ref[0] = proj(_BA)


def _inproj(x, mod, norm_g, w_packed, dn_conv_w, layer, tm, row_len, moe=None, mod_prev=None):
    b, l, d = x.shape
    assert tm % row_len == 0
    has_moe = moe is not None
    tok = pl.BlockSpec((1, tm, d), lambda i, j: (i, j, 0))
    modspec = pl.BlockSpec((1, 8, d), lambda i, j: (i, 0, 0))
    in_specs = [tok]
    args = [x]
    if has_moe:
        in_specs += [tok, modspec]
        args += [moe, mod_prev]
    in_specs += [modspec,
                 pl.BlockSpec((1, d), lambda i, j: (0, 0)),
                 pl.BlockSpec((1, d, N_PACKED), lambda i, j: (layer, 0, 0), pipeline_mode=pl.Buffered(1)),
                 pl.BlockSpec((1,) + dn_conv_w.shape[1:], lambda i, j: (layer, 0, 0))]
    args += [mod, norm_g.reshape(1, d), w_packed, dn_conv_w]
    widths = [(_QKV[1], BF16), (_Z[1], BF16), (_HY[1], BF16), (_SC[1], BF16), (_GATES[1], BF16), (_BA[1], F32)]
    out_specs = [pl.BlockSpec((1, tm, w), lambda i, j: (i, j, 0)) for w, _ in widths]
    out_shape = [jax.ShapeDtypeStruct((b, l, w), dt) for w, dt in widths]
    if has_moe:
        out_specs.append(tok)
        out_shape.append(jax.ShapeDtypeStruct((b, l, d), F32))
    return pl.pallas_call(
        functools.partial(_inproj_kernel, has_moe=has_moe, row_len=row_len),
        grid=(b, l // tm),
        in_specs=in_specs,
        out_specs=out_specs,
        out_shape=out_shape,
        compiler_params=_cparams(("parallel", "parallel")),
        name="inproj",
    )(*args)


_BETA_LANE = 0
_DECAY_LANE = 2 * DN_HEADS
_BWD_DECAY_LANE = _DECAY_LANE + DN_HEADS


def _dn_gates_kernel(ba_ref, alog_ref, dtb_ref, o_ref, *, nc):
    x = ba_ref[0]
    beta = _sigmoid(x)
    g = -jnp.exp(alog_ref[...]) * _softplus(x + dtb_ref[...])
    ii = lax.broadcasted_iota(jnp.int32, (DN_CHUNK, DN_CHUNK), 0)
    jj = lax.broadcasted_iota(jnp.int32, (DN_CHUNK, DN_CHUNK), 1)
    lower = jnp.where(ii >= jj, 1.0, 0.0).astype(F32)
    upper = jnp.where(ii <= jj, 1.0, 0.0).astype(F32)
    lane = lax.broadcasted_iota(jnp.int32, (DN_CHUNK, LANES), 1)
    for c in range(nc):
        rows = slice(c * DN_CHUNK, (c + 1) * DN_CHUNK)
        gch = g[rows]
        acc = jnp.where(lane >= _BWD_DECAY_LANE, _mm_f32(upper, gch), _mm_f32(lower, gch))
        o_ref[0, rows, :] = jnp.where(lane < _DECAY_LANE, beta[rows], acc)


def _dn_gates(ba, a_log, dt_bias):
    b, l, _ = ba.shape
    nc = l // DN_CHUNK
    vec = lambda p: jnp.zeros((1, LANES), F32).at[0, _DECAY_LANE:_DECAY_LANE + 2 * DN_HEADS].set(p.reshape(-1))
    return pl.pallas_call(
        functools.partial(_dn_gates_kernel, nc=nc),
        grid=(b,),
        in_specs=[pl.BlockSpec((1, l, LANES), lambda i: (i, 0, 0)),
                  pl.BlockSpec((1, LANES), lambda i: (0, 0)),
                  pl.BlockSpec((1, LANES), lambda i: (0, 0))],
        out_specs=pl.BlockSpec((1, l, LANES), lambda i: (i, 0, 0)),
        out_shape=jax.ShapeDtypeStruct((b, l, LANES), F32),
        compiler_params=_cparams(("parallel",)),
        name="dn_gates",
    )(ba, vec(a_log), vec(dt_bias))


_GROUP_CHUNKS = 4


def _dn_kernel(qkv_ref, z_ref, gb_ref, gbt_ref, ng_ref, s0f_ref, s0b_ref, ya_ref, sf_ref, sb_ref,
               of_s, ob_s, st_s, *, ngroups, gc):
    c, hd, nh = DN_CHUNK, DN_HEAD_DIM, DN_HEADS
    gt = gc * c
    shift = c.bit_length() - 1

    for h in range(nh):
        st_s[h] = s0f_ref[0, h]
        st_s[nh + h] = s0b_ref[0, h]

    rr = lax.broadcasted_iota(jnp.int32, (c, gt), 0)
    ln = lax.broadcasted_iota(jnp.int32, (c, gt), 1)
    lblk = ln >> shift
    cc = ln & (c - 1)
    eye_p = jnp.where(rr == cc, 1.0, 0.0).astype(F32)
    inside = [(rr >> lv) == (cc >> lv) for lv in range(1, shift + 1)]
    level_masks = [jnp.where(inside[0], 1.0, 0.0).astype(F32)] + [
        jnp.where(jnp.logical_and(inside[j], jnp.logical_not(inside[j - 1])), 1.0, 0.0).astype(F32)
        for j in range(1, shift)]
    same_blk = ((lax.broadcasted_iota(jnp.int32, (gt, gt), 0) >> shift)
                == (lax.broadcasted_iota(jnp.int32, (gt, gt), 1) >> shift))
    blk_mask = jnp.where(same_blk, 1.0, 0.0).astype(BF16)
    rblk = lax.broadcasted_iota(jnp.int32, (gt, 1), 0) >> shift

    def pack_cols(col):
        out = jnp.broadcast_to(col[0:c], (c, gt))
        for i in range(1, gc):
            out = jnp.where(lblk == i, jnp.broadcast_to(col[i * c:(i + 1) * c], (c, gt)), out)
        return out

    def pack_diag(full):
        out = full[0:c]
        for i in range(1, gc):
            out = jnp.where(lblk == i, full[i * c:(i + 1) * c], out)
        return out

    def block_diag(packed_bf16):
        return jnp.concatenate([packed_bf16] * gc, axis=0) * blk_mask

    pair_lane = lax.broadcasted_iota(jnp.int32, (1, 2 * hd), 1)
    first_head = jnp.where(pair_lane < hd, 1.0, 0.0).astype(BF16)
    second_head = jnp.where(pair_lane >= hd, 1.0, 0.0).astype(BF16)

    def pair_diag(pair_bf16):
        return jnp.concatenate([pair_bf16 * first_head, pair_bf16 * second_head], axis=0)

    def dotf(a, b):
        return jnp.dot(a, b, preferred_element_type=F32)

    def prepare(h, d, g):
        r0 = pl.multiple_of(g * gt, gt)
        part = lambda p: qkv_ref[0, pl.ds(r0, gt), (p * nh + h) * hd:(p * nh + h + 1) * hd].astype(F32)
        qg, kg, vg = part(0), part(1), part(2)
        gbg = gb_ref[0, pl.ds(r0, gt), :]
        col = d * nh + h
        beta = gbg[:, _BETA_LANE + col:_BETA_LANE + col + 1]
        gcc = gbg[:, _DECAY_LANE + col:_DECAY_LANE + col + 1]
        gcr = gbt_ref[0, g][_DECAY_LANE + col:_DECAY_LANE + col + 1, :]
        if d == 0:
            incl, strict, last = rr >= cc, rr > cc, c - 1
        else:
            incl, strict, last = rr <= cc, rr < cc, 0
        kq = _mm_nt(jnp.concatenate([kg, qg], axis=0), kg)
        decay = jnp.where(incl, jnp.exp(jnp.where(incl, pack_cols(gcc) - gcr, 0.0)), 0.0)
        m = jnp.where(strict, pack_diag(kq[:gt]) * pack_cols(beta) * decay, 0.0)
        attn = pack_diag(kq[gt:]) * decay
        eg = jnp.exp(gcc)
        glast = jnp.broadcast_to(gcc[last:last + 1], (gt, 1))
        for i in range(1, gc):
            glast = jnp.where(rblk == i, gcc[i * c + last:i * c + last + 1], glast)
        return dict(m=m, attn=attn.astype(BF16), qd=(qg * eg).astype(BF16),
                    kd=(kg * jnp.exp(glast - gcc)).astype(BF16),
                    rhs=jnp.concatenate([vg * beta, kg * (beta * eg)], axis=1).astype(BF16),
                    chunk_decay=[jnp.exp(gcc[i * c + last:i * c + last + 1]) for i in range(gc)])

    def body(j, carry):
        groups = (j, ngroups - 1 - j)
        slots = [(h, d) for h in range(nh) for d in range(2)]
        state = [st_s[d * nh + h] for h, d in slots]
        pre = [prepare(h, d, groups[d]) for h, d in slots]
        ts = [eye_p - p['m'] * level_masks[0] for p in pre]
        for level_mask in level_masks[1:]:
            xs = [dotf((p['m'] * level_mask).astype(BF16), block_diag(t.astype(BF16))) for p, t in zip(pre, ts)]
            ts = [t - dotf(t.astype(BF16), block_diag(x.astype(BF16))) for t, x in zip(ts, xs)]
        uws = [dotf(block_diag(t.astype(BF16)), p['rhs']) for p, t in zip(pre, ts)]
        lanes = lambda parts: jnp.concatenate(parts, axis=1)
        index = {slot: n for n, slot in enumerate(slots)}
        pairs = [(hp, d) for hp in range(nh // 2) for d in range(2)]
        packed = []
        for hp, d in pairs:
            na, nb = index[(2 * hp, d)], index[(2 * hp + 1, d)]
            packed.append(dict(
                s=lanes([state[na], state[nb]]),
                u=lanes([uws[na][:, :hd], uws[nb][:, :hd]]),
                w=lanes([uws[na][:, hd:], uws[nb][:, hd:]]).astype(BF16),
                qd=lanes([pre[na]['qd'], pre[nb]['qd']]), a=pre[na], b=pre[nb]))
        outs = [[None] * gc for _ in pairs]
        for step in range(gc):
            for n, (hp, d) in enumerate(pairs):
                i = step if d == 0 else gc - 1 - step
                rows = slice(i * c, (i + 1) * c)
                q = packed[n]
                wq = jnp.concatenate([q['w'][rows], q['qd'][rows]], axis=0)
                a = dotf(wq, pair_diag(q['s'].astype(BF16)))
                v_new = (q['u'][rows] - a[:c]).astype(BF16)
                lhs = jnp.concatenate([lanes([q['a']['attn'][:, rows], q['b']['attn'][:, rows]]),
                                       lanes([q['a']['kd'][rows].T, q['b']['kd'][rows].T])], axis=0)
                av = dotf(lhs, pair_diag(v_new))
                outs[n][i] = a[c:] + av[:c]
                decay = lanes([jnp.broadcast_to(q['a']['chunk_decay'][i], (1, hd)),
                               jnp.broadcast_to(q['b']['chunk_decay'][i], (1, hd))])
                q['s'] = q['s'] * decay + av[c:]
        for n, (hp, d) in enumerate(pairs):
            st_s[d * nh + 2 * hp] = packed[n]['s'][:, :hd]
            st_s[d * nh + 2 * hp + 1] = packed[n]['s'][:, hd:]
            r0 = pl.multiple_of(groups[d] * gt, gt)
            (of_s, ob_s)[d][pl.ds(r0, gt), 2 * hp * hd:(2 * hp + 2) * hd] = jnp.concatenate(outs[n], axis=0)
        return carry

    lax.fori_loop(0, ngroups, body, 0)
    for h in range(nh):
        sf_ref[0, h] = st_s[h]
        sb_ref[0, h] = st_s[nh + h]
        hs = slice(h * hd, (h + 1) * hd)
        o = of_s[:, hs] + ob_s[:, hs]
        y = o * lax.rsqrt(jnp.mean(o * o, axis=-1, keepdims=True) + NORM_EPS) * ng_ref[...]
        ya_ref[0, :, hs] = (y * _silu(z_ref[0, :, hs].astype(F32))).astype(BF16)


def _deltanet(qkv, z, gb, norm_g, s0f, s0b):
    b, l, _ = qkv.shape
    c, hd, nh = DN_CHUNK, DN_HEAD_DIM, DN_HEADS
    gc = min(_GROUP_CHUNKS, l // c)
    gt = gc * c
    ngroups = l // gt
    nlanes = 4 * nh
    gbt = jnp.transpose(gb[:, :, :nlanes].reshape(b, ngroups, gt, nlanes), (0, 1, 3, 2))
    st_blk = pl.BlockSpec((1, nh, hd, hd), lambda i: (i, 0, 0, 0))
    seq_blk = lambda w: pl.BlockSpec((1, l, w), lambda i: (i, 0, 0))
    return pl.pallas_call(
        functools.partial(_dn_kernel, ngroups=ngroups, gc=gc),
        grid=(b,),
        in_specs=[seq_blk(3 * DN_WIDTH), seq_blk(DN_WIDTH), seq_blk(LANES),
                  pl.BlockSpec((1, ngroups, nlanes, gt), lambda i: (i, 0, 0, 0)),
                  pl.BlockSpec((1, hd), lambda i: (0, 0)),
                  st_blk, st_blk],
        out_specs=[seq_blk(DN_WIDTH), st_blk, st_blk],
        out_shape=[jax.ShapeDtypeStruct((b, l, DN_WIDTH), BF16),
                   jax.ShapeDtypeStruct((b, nh, hd, hd), F32),
                   jax.ShapeDtypeStruct((b, nh, hd, hd), F32)],
        scratch_shapes=[pltpu.VMEM((l, DN_WIDTH), F32),
                        pltpu.VMEM((l, DN_WIDTH), F32),
                        pltpu.VMEM((2 * nh, hd, hd), F32)],
        compiler_params=_cparams(("parallel",)),
        name="deltanet",
    )(qkv, z, gb, gbt, norm_g.reshape(1, hd), s0f, s0b)


def _dft_tables(l):
    n = 2 * l
    nb = GRID_W
    f = jnp.arange(l, dtype=jnp.int32)[:, None]
    angle = lambda k: (k % n).astype(F32) * (2.0 * math.pi / n)
    ang_a = angle(f * (nb * jnp.arange(l // nb, dtype=jnp.int32))[None, :])[:, :, None]
    ang_b = angle(f * jnp.arange(nb, dtype=jnp.int32)[None, :])[:, None, :]
    ca, sa, cb, sb = jnp.cos(ang_a), jnp.sin(ang_a), jnp.cos(ang_b), jnp.sin(ang_b)
    return ((ca * cb - sa * sb).reshape(l, l).astype(BF16), (sa * cb + ca * sb).reshape(l, l).astype(BF16))


def _hy_features(l):
    pos = jnp.arange(l, dtype=F32)
    t = (pos / max(l - 1, 1))[:, None]
    bands = jnp.linspace(1e-4, HY_BANDS - 1, HY_BANDS, dtype=F32)
    ang = (2.0 * math.pi / l) * pos[:, None] * bands[None, :]
    feats = jnp.concatenate([t, jnp.cos(ang), -jnp.sin(ang)], axis=-1)
    feats = jnp.pad(feats, ((0, 0), (0, LANES - feats.shape[1])))
    deltas = jnp.abs(jnp.linspace(math.log(HY_TARGET) / HY_SLOW, math.log(HY_TARGET) / HY_FAST,
                                  HY_WIDTH, dtype=F32))
    return feats, t, deltas[None, :]


def _hy_filter_kernel(feats_ref, t_ref, deltas_ref, w1_ref, b1_ref, w2_ref, b2_ref, w3_ref, freq_ref,
                      cos_ref, sin_ref, ka_ref, kb_ref, kn_ref):
    l = feats_ref.shape[0]
    n = 2 * l
    freq = freq_ref[...]
    hh = jnp.sin(freq * (_mm_f32(feats_ref[...], w1_ref[...]) + b1_ref[...]))
    hh = jnp.sin(freq * (_mm_f32(hh, w2_ref[...]) + b2_ref[...]))
    hh = _mm_f32(hh, w3_ref[...])
    window = jnp.exp(-t_ref[...] * deltas_ref[...])
    h_fwd = hh[:, :HY_WIDTH] * window
    h_bwd = hh[:, HY_WIDTH:] * window
    row = lax.broadcasted_iota(jnp.int32, (l, HY_WIDTH), 0)
    h_bwd = jnp.where(row == 0, 0.0, h_bwd)
    hs = h_fwd + h_bwd
    hd = h_fwd - h_bwd
    hs_hi, hs_lo = _split_bf16(hs)
    hd_hi, hd_lo = _split_bf16(hd)
    blk = min(_DFT_ROW_BLOCK, l)
    brow = lax.broadcasted_iota(jnp.int32, (blk, HY_WIDTH), 0)
    for i in range(l // blk):
        rows = slice(i * blk, (i + 1) * blk)
        cos_b = cos_ref[rows, :]
        sin_b = sin_ref[rows, :]
        ka = jnp.dot(cos_b, hs_hi, preferred_element_type=F32) + jnp.dot(cos_b, hs_lo, preferred_element_type=F32)
        kb = jnp.dot(sin_b, hd_hi, preferred_element_type=F32) + jnp.dot(sin_b, hd_lo, preferred_element_type=F32)
        ka_ref[rows, :] = ka * (jnp.where(brow == 0, 1.0 / n, 2.0 / n) if i == 0 else 2.0 / n)
        kb_ref[rows, :] = kb * (2.0 / n)
    nyq = jnp.sum(jnp.where((row & 1) == 0, hs, -hs), axis=0, keepdims=True) * (1.0 / n)
    kn_ref[...] = jnp.broadcast_to(nyq, kn_ref.shape)


def _hy_filter(l, lp, tables):
    cos_t, sin_t, feats, t, deltas = tables
    w1 = jnp.pad(lp['hy_w1'], ((0, LANES - lp['hy_w1'].shape[0]), (0, 0)))
    row = lambda v: v.reshape(1, -1)
    args = (feats, t, deltas, w1, row(lp['hy_b1']), lp['hy_w2'], row(lp['hy_b2']), lp['hy_w3'],
            row(lp['hy_freq']), cos_t, sin_t)
    return pl.pallas_call(
        _hy_filter_kernel,
        out_shape=[jax.ShapeDtypeStruct((l, HY_WIDTH), F32),
                   jax.ShapeDtypeStruct((l, HY_WIDTH), F32),
                   jax.ShapeDtypeStruct((8, HY_WIDTH), F32)],
        compiler_params=pltpu.CompilerParams(vmem_limit_bytes=VMEM_LIMIT),
        name="hyena_filter",
    )(*args)


_DFT_ROW_BLOCK = 256


def _hyena_kernel(hy_ref, cw_ref, cb_ref, db_ref, cos_ref, sin_ref, ka_ref, kb_ref, kn_ref, o_ref, *, row_len):
    l = hy_ref.shape[1]
    u = _conv3(hy_ref[0].astype(F32), cw_ref[...], row_len) + cb_ref[...]
    x0 = u[:, :HY_WIDTH]
    z = u[:, HY_WIDTH:2 * HY_WIDTH] * u[:, 2 * HY_WIDTH:]
    zb = z.astype(BF16)
    blk = min(_DFT_ROW_BLOCK, l)
    blocks = [slice(i * blk, (i + 1) * blk) for i in range(l // blk)]
    ya, yb = [], []
    for rows in blocks:
        xa = jnp.dot(cos_ref[rows, :], zb, preferred_element_type=F32)
        xb = jnp.dot(sin_ref[rows, :], zb, preferred_element_type=F32)
        ka = ka_ref[rows, :]
        kb = kb_ref[rows, :]
        ya.append((xa * ka - xb * kb).astype(BF16))
        yb.append((xa * kb + xb * ka).astype(BF16))
    ya = jnp.concatenate(ya, axis=0)
    yb = jnp.concatenate(yb, axis=0)
    even = (lax.broadcasted_iota(jnp.int32, (blk, HY_WIDTH), 0) & 1) == 0
    sign = jnp.where((lax.broadcasted_iota(jnp.int32, (l, HY_WIDTH), 0) & 1) == 0, 1.0, -1.0)
    y_nyq = jnp.sum(z * sign, axis=0, keepdims=True) * kn_ref[0:1]
    for rows in blocks:
        y = (jnp.dot(cos_ref[rows, :], ya, preferred_element_type=F32)
             + jnp.dot(sin_ref[rows, :], yb, preferred_element_type=F32))
        y = y + jnp.where(even, y_nyq, -y_nyq)
        o_ref[0, rows, :] = (x0[rows] * (y + z[rows] * db_ref[...])).astype(BF16)


def _hyena(hy, lp, filt, tables, row_len):
    b, l, w3 = hy.shape
    ka, kb, kn = filt
    cos_t, sin_t = tables[0], tables[1]
    const = lambda shape: pl.BlockSpec(shape, lambda i: (0,) * len(shape), pipeline_mode=pl.Buffered(1))
    return pl.pallas_call(
        functools.partial(_hyena_kernel, row_len=row_len),
        grid=(b,),
        in_specs=[pl.BlockSpec((1, l, w3), lambda i: (i, 0, 0)),
                  const((3, w3)), const((1, w3)), const((1, HY_WIDTH)),
                  const((l, l)), const((l, l)),
                  const((l, HY_WIDTH)), const((l, HY_WIDTH)), const((8, HY_WIDTH))],
        out_specs=pl.BlockSpec((1, l, HY_WIDTH), lambda i: (i, 0, 0)),
        out_shape=jax.ShapeDtypeStruct((b, l, HY_WIDTH), BF16),
        compiler_params=_cparams(("parallel",)),
        name="hyena",
    )(hy, lp['hy_conv_w'], lp['hy_conv_b'].reshape(1, w3), lp['hy_d_bias'].reshape(1, HY_WIDTH),
      cos_t, sin_t, ka, kb, kn)


def _merge_kernel(x_ref, ya_ref, yb_ref, sc_ref, gates_ref, mod_ref, n2_ref, scw_ref, wa_ref, wb_ref,
                  wc_ref, wo_ref, wr_ref, xo_ref, h2_ref, lg_ref, *, row_len, parts):
    tm, d = x_ref.shape[1:]
    m = mod_ref[0]
    w_hi, w_lo = _split_bf16(wr_ref[0])
    for part in range(parts):
        rows = slice(part * (tm // parts), (part + 1) * (tm // parts))
        sc = sc_ref[0, rows, :].astype(F32)
        yc = sc[:, :SC_WIDTH] * _conv3(sc[:, SC_WIDTH:2 * SC_WIDTH] * sc[:, 2 * SC_WIDTH:], scw_ref[...], row_len)
        gate = lambda j: _sigmoid(gates_ref[0, rows, j * d:(j + 1) * d].astype(F32))
        merged = (gate(0) * jnp.dot(ya_ref[0, rows, :], wa_ref[0], preferred_element_type=F32)
                  + gate(1) * jnp.dot(yb_ref[0, rows, :], wb_ref[0], preferred_element_type=F32)
                  + gate(2) * _mm(yc, wc_ref[0]))
        x = x_ref[0, rows, :] + m[2:3] * _mm(merged, wo_ref[0])
        xo_ref[0, rows, :] = x
        h2 = _rms_mod(x, n2_ref[...], m[4:5], m[3:4])
        h2_ref[0, rows, :] = h2.astype(BF16)
        h_hi, h_lo = _split_bf16(h2)
        lg_ref[0, rows, :] = (jnp.dot(h_hi, w_hi, preferred_element_type=F32)
                              + jnp.dot(h_lo, w_hi, preferred_element_type=F32)
                              + jnp.dot(h_hi, w_lo, preferred_element_type=F32))


_MERGE_PART_ROWS = 256


def _merge(x, ya, yb, sc, gates, mod, lp, wts, layer, tm, row_len):
    b, l, d = x.shape
    parts = max(1, tm // max(row_len, _MERGE_PART_ROWS))
    tok = lambda w: pl.BlockSpec((1, tm, w), lambda i, j: (i, j, 0))
    const = lambda shape: pl.BlockSpec(shape, lambda i, j: (0,) * len(shape), pipeline_mode=pl.Buffered(1))
    per_layer = lambda r, c: pl.BlockSpec((1, r, c), lambda i, j: (layer, 0, 0), pipeline_mode=pl.Buffered(1))
    return pl.pallas_call(
        functools.partial(_merge_kernel, row_len=row_len, parts=parts),
        grid=(b, l // tm),
        in_specs=[tok(d), tok(DN_WIDTH), tok(HY_WIDTH), tok(3 * SC_WIDTH), tok(N_BRANCH * d),
                  pl.BlockSpec((1, 8, d), lambda i, j: (i, 0, 0)),
                  const((1, d)), const((3, SC_WIDTH)),
                  per_layer(DN_WIDTH, d), per_layer(HY_WIDTH, d), per_layer(SC_WIDTH, d), per_layer(d, d),
                  per_layer(d, LANES)],
        out_specs=[tok(d), tok(d), tok(LANES)],
        out_shape=[jax.ShapeDtypeStruct((b, l, d), F32),
                   jax.ShapeDtypeStruct((b, l, d), BF16),
                   jax.ShapeDtypeStruct((b, l, LANES), F32)],
        compiler_params=_cparams(("parallel", "parallel")),
        name="merge",
    )(x, ya, yb, sc, gates, mod, lp['norm2_g'].reshape(1, d), lp['sc_conv_w'],
      wts['w_a'], wts['w_b'], wts['w_c'], wts['w_o'], wts['w_r'])


_PREFIX_BLOCK = 256


def _router_kernel(lg_ref, slot_ref, aff_ref, slott_ref, afft_ref, *, cap):
    l = lg_ref.shape[1]
    lane = lax.broadcasted_iota(jnp.int32, (l, LANES), 1)
    valid = lane < N_EXPERTS
    lg = jnp.where(valid, lg_ref[0], -jnp.inf)
    e = jnp.where(valid, jnp.exp(lg - jnp.max(lg, axis=-1, keepdims=True)), 0.0)
    aff = e / jnp.sum(e, axis=-1, keepdims=True)

    def count(mask):
        return jnp.sum(jnp.where(mask, 1.0, 0.0), axis=0, keepdims=True)

    aff_t = aff.T
    dense = aff_t[:N_EXPERTS]

    def bisect(i, lo):
        cand = lo | jnp.left_shift(jnp.int32(1), 30 - i)
        n_ge = jnp.sum(jnp.where(dense >= pltpu.bitcast(cand, F32)[:, :1], 1.0, 0.0), axis=-1, keepdims=True)
        return jnp.where(n_ge >= cap, cand, lo)

    thr_t = pltpu.bitcast(lax.fori_loop(0, 31, bisect, jnp.zeros((N_EXPERTS, LANES), jnp.int32)), F32)
    on_diag = (lax.broadcasted_iota(jnp.int32, (N_EXPERTS, LANES), 0)
               == lax.broadcasted_iota(jnp.int32, (N_EXPERTS, LANES), 1))
    thr = jnp.sum(jnp.where(on_diag, thr_t, 0.0), axis=0, keepdims=True)
    above = aff > thr
    tie = aff == thr
    need = cap - count(above)

    pb = _PREFIX_BLOCK if l % _PREFIX_BLOCK == 0 else l
    ii = lax.broadcasted_iota(jnp.int32, (pb, pb), 0)
    jj = lax.broadcasted_iota(jnp.int32, (pb, pb), 1)
    before = jnp.where(ii > jj, 1.0, 0.0).astype(BF16)

    def prefix(mask):
        vals = jnp.where(mask, 1.0, 0.0)
        out = []
        offset = jnp.zeros((1, LANES), F32)
        for r in range(l // pb):
            blk = vals[r * pb:(r + 1) * pb]
            out.append(jnp.dot(before, blk.astype(BF16), preferred_element_type=F32) + offset)
            offset = offset + jnp.sum(blk, axis=0, keepdims=True)
        return jnp.concatenate(out, axis=0)

    chosen = jnp.logical_or(above, jnp.logical_and(tie, prefix(tie) < need))
    chosen = jnp.logical_and(chosen, valid)
    slot = jnp.where(chosen, prefix(chosen), -1.0)
    slot_ref[0] = slot
    aff_ref[0] = aff
    slott_ref[0] = slot.T
    afft_ref[0] = aff_t


def _router(logits, cap):
    b, l, _ = logits.shape
    blk = pl.BlockSpec((1, l, LANES), lambda i: (i, 0, 0))
    blk_t = pl.BlockSpec((1, LANES, l), lambda i: (i, 0, 0))
    return pl.pallas_call(
        functools.partial(_router_kernel, cap=cap),
        grid=(b,),
        in_specs=[blk],
        out_specs=[blk, blk, blk_t, blk_t],
        out_shape=[jax.ShapeDtypeStruct((b, l, LANES), F32), jax.ShapeDtypeStruct((b, l, LANES), F32),
                   jax.ShapeDtypeStruct((b, LANES, l), F32), jax.ShapeDtypeStruct((b, LANES, l), F32)],
        compiler_params=_cparams(("parallel",)),
        name="router",
    )(logits)


_SCATTER_ROW_BLOCK = 256


def _moe_kernel(h_ref, slot_ref, aff_ref, wg_ref, wu_ref, wd_ref, o_ref, *, cap):
    e = pl.program_id(1)
    bs, l, _ = h_ref.shape
    want = lax.broadcasted_iota(jnp.int32, (cap, l), 0).astype(F32)
    onehots, vals, xes = [], [], []
    for s in range(bs):
        hit = want == slot_ref[s, 0]
        onehots.append(jnp.where(hit, 1.0, 0.0).astype(BF16))
        vals.append(jnp.sum(jnp.where(hit, aff_ref[s, 0], 0.0), axis=-1, keepdims=True))
        xes.append(jnp.dot(onehots[s], h_ref[s], preferred_element_type=F32).astype(BF16))
    xe = jnp.concatenate(xes, axis=0)
    hid = (_silu(jnp.dot(xe, wg_ref[0, 0].astype(BF16), preferred_element_type=F32))
           * jnp.dot(xe, wu_ref[0, 0].astype(BF16), preferred_element_type=F32)).astype(BF16)
    ye = jnp.dot(hid, wd_ref[0, 0].astype(BF16), preferred_element_type=F32) * jnp.concatenate(vals, axis=0)
    tb = min(_SCATTER_ROW_BLOCK, l)

    def scatter(first):
        for s in range(bs):
            yw = ye[s * cap:(s + 1) * cap].astype(BF16)
            for r in range(l // tb):
                rows = slice(r * tb, (r + 1) * tb)
                contrib = lax.dot_general(onehots[s][:, rows], yw, (((0,), (0,)), ((), ())),
                                          preferred_element_type=F32)
                if first:
                    o_ref[s, rows, :] = contrib
                else:
                    o_ref[s, rows, :] += contrib

    pl.when(e == 0)(functools.partial(scatter, True))
    pl.when(e != 0)(functools.partial(scatter, False))


def _moe(h2, slot_t, aff_t, wts, layer, cap, bs):
    b, l, d = h2.shape
    _, ne, _, ff = wts['w_gate'].shape
    row = lambda a: a[:, :ne].reshape(b, ne, 1, l)
    rspec = pl.BlockSpec((bs, 1, 1, l), lambda i, j: (i, j, 0, 0))
    tok = pl.BlockSpec((bs, l, d), lambda i, j: (i, 0, 0))
    wspec = lambda r, c: pl.BlockSpec((1, 1, r, c), lambda i, j: (layer, j, 0, 0))
    return pl.pallas_call(
        functools.partial(_moe_kernel, cap=cap),
        grid=(b // bs, ne),
        in_specs=[tok, rspec, rspec, wspec(d, ff), wspec(d, ff), wspec(ff, d)],
        out_specs=tok,
        out_shape=jax.ShapeDtypeStruct((b, l, d), F32),
        compiler_params=_cparams(("parallel", "arbitrary")),
        name="experts",
    )(h2, row(slot_t), row(aff_t), wts['w_gate'], wts['w_up'], wts['w_down'])


def _final_kernel(x_ref, moe_ref, mod_ref, g_ref, o_ref):
    x = x_ref[0] + mod_ref[0][5:6] * moe_ref[0]
    o_ref[0] = x * lax.rsqrt(jnp.mean(x * x, axis=-1, keepdims=True) + NORM_EPS) * g_ref[...]


def _final(x, moe, mod, g, tm):
    b, l, d = x.shape
    tok = pl.BlockSpec((1, tm, d), lambda i, j: (i, j, 0))
    return pl.pallas_call(
        _final_kernel,
        grid=(b, l // tm),
        in_specs=[tok, tok, pl.BlockSpec((1, 8, d), lambda i, j: (i, 0, 0)),
                  pl.BlockSpec((1, d), lambda i, j: (0, 0))],
        out_specs=tok,
        out_shape=jax.ShapeDtypeStruct((b, l, d), F32),
        compiler_params=_cparams(("parallel", "parallel")),
        name="final_norm",
    )(x, moe, mod, g.reshape(1, d))


def _token_block(l):
    return min(l, 512)


_EXPERT_TOKENS_PER_STEP = 2048


def _samples_per_step(b, l):
    bs = max(1, min(b, _EXPERT_TOKENS_PER_STEP // l))
    while b % bs:
        bs -= 1
    return bs


def kernel(x, c, ctx, c_ctx, ada_w, ada_b, norm1_g, norm2_g, w_in, dn_conv_w, dn_a_log, dn_dt_bias, dn_norm_g, hy_conv_w, hy_conv_b, hy_w1, hy_b1, hy_w2, hy_b2, hy_w3, hy_freq, hy_d_bias, sc_conv_w, w_branch_a, w_branch_b, w_branch_c, w_out, router_w, exp_w_gate, exp_w_up, exp_w_down, final_norm_g):
    depth = ada_w.shape[0]
    b, l, d = x.shape
    lc = ctx.shape[1]
    tm_x, tm_c = _token_block(l), _token_block(lc)
    cap_x = CAPACITY_FACTOR * l // N_EXPERTS
    cap_c = CAPACITY_FACTOR * lc // N_EXPERTS

    rows = -(-(b + 1) // 8) * 8
    cvec = jnp.zeros((rows, d), F32).at[:b].set(c).at[b].set(c_ctx)
    mod_all = _modulation(cvec, ada_w, ada_b).reshape(depth, rows, 6, d)
    mod_all = jnp.pad(mod_all, ((0, 0), (0, 0), (0, 2), (0, 0)))
    mods_x = [mod_all[i, :b] for i in range(depth)]
    mods_c = [jnp.broadcast_to(mod_all[i, b], (b, 8, d)) for i in range(depth)]

    tables_x = _dft_tables(l) + _hy_features(l)
    tables_c = _dft_tables(lc) + _hy_features(lc)
    zero_state = jnp.zeros((b, DN_HEADS, DN_HEAD_DIM, DN_HEAD_DIM), F32)

    w_packed = _pack_w_in(w_in)
    wts = {
        'w_a': w_branch_a.astype(BF16), 'w_b': w_branch_b.astype(BF16), 'w_c': w_branch_c.astype(BF16),
        'w_o': w_out.astype(BF16), 'w_r': jnp.pad(router_w, ((0, 0), (0, 0), (0, LANES - N_EXPERTS))),
        'w_gate': exp_w_gate, 'w_up': exp_w_up, 'w_down': exp_w_down,
    }

    moe_x = moe_c = None
    for i in range(depth):
        lp = {
            'dn_a_log': dn_a_log[i], 'dn_dt_bias': dn_dt_bias[i],
            'dn_norm_g': dn_norm_g[i], 'hy_conv_w': hy_conv_w[i], 'hy_conv_b': hy_conv_b[i],
            'hy_w1': hy_w1[i], 'hy_b1': hy_b1[i], 'hy_w2': hy_w2[i], 'hy_b2': hy_b2[i], 'hy_w3': hy_w3[i],
            'hy_freq': hy_freq[i], 'hy_d_bias': hy_d_bias[i], 'sc_conv_w': sc_conv_w[i],
            'norm2_g': norm2_g[i],
        }
        last = i == depth - 1

        def mixer_inputs(tokens, moe, mods, tm, row_len):
            if moe is None:
                outs = _inproj(tokens, mods[i], norm1_g[i], w_packed, dn_conv_w, i, tm, row_len)
                return tokens, outs
            outs = _inproj(tokens, mods[i], norm1_g[i], w_packed, dn_conv_w, i, tm, row_len,
                           moe=moe, mod_prev=mods[i - 1])
            return outs[-1], outs[:-1]

        def rest_of_layer(tokens, parts, ya, mods, tm, row_len, tables, cap, samples_per_step):
            _, _, hy, sc, gates, _ = parts
            filt = _hy_filter(tokens.shape[1], lp, tables)
            yb = _hyena(hy, lp, filt, tables, row_len)
            tokens, h2, logits = _merge(tokens, ya, yb, sc, gates, mods[i], lp, wts, i, tm, row_len)
            _, _, slot_t, aff_t = _router(logits, cap)
            return tokens, _moe(h2, slot_t, aff_t, wts, i, cap, samples_per_step)

        ctx, parts_c = mixer_inputs(ctx, moe_c, mods_c, tm_c, lc)
        gb_c = _dn_gates(parts_c[5], lp['dn_a_log'], lp['dn_dt_bias'])
        ya_c, s_f, s_b = _deltanet(parts_c[0], parts_c[1], gb_c, lp['dn_norm_g'], zero_state, zero_state)
        if not last:
            ctx, moe_c = rest_of_layer(ctx, parts_c, ya_c, mods_c, tm_c, lc, tables_c, cap_c, _samples_per_step(b, lc))

        x, parts_x = mixer_inputs(x, moe_x, mods_x, tm_x, GRID_W)
        gb_x = _dn_gates(parts_x[5], lp['dn_a_log'], lp['dn_dt_bias'])
        ya_x, _, _ = _deltanet(parts_x[0], parts_x[1], gb_x, lp['dn_norm_g'], s_f, s_b)
        x, moe_x = rest_of_layer(x, parts_x, ya_x, mods_x, tm_x, GRID_W, tables_x, cap_x, _samples_per_step(b, l))

    return _final(x, moe_x, mods_x[depth - 1], final_norm_g, tm_x)
```
